```python
import jax, jax.numpy as jnp
from jax import lax
import numpy as np

D_MODEL = 1024
BATCH = 8
SEQ = 2048
DEPTH = 1
DEC_BATCH = 128
DEC_SEQ = 8
PAST_LEN = 8192
PAGE_SIZE = 128

ROPE_THETA = 500000.0
EPS = 1e-6
H_A = 8
HKV_A = 2
DH_A = 64
ROT_A = DH_A // 4
H_IDX = 4
D_IDX = 64
ROT_IDX = D_IDX // 4
TOPK_MAX = 256
H_B = 8
DN_B = 64
DR_B = 32
DV_B = 64
D_CQ = 384
D_C = 256
PEER_HEADS = 8
N_KEYS = 128
N_EXPERTS = N_KEYS * N_KEYS
D_KEY = 128
PEER_TOPK = 16
Q_BLOCK = 128
PEER_BLOCK = 128

IN_WIDTHS = (H_A * DH_A, HKV_A * DH_A, HKV_A * DH_A, H_IDX * D_IDX, D_IDX, H_IDX, D_CQ, D_C, DR_B)
IN_OFFSETS = tuple(sum(IN_WIDTHS[:i + 1]) for i in range(len(IN_WIDTHS) - 1))
D_IN = sum(IN_WIDTHS)
D_MIX = H_A * DH_A + H_B * DV_B

kernel_name = "hymba_dsa_mla_peer_step"


def rms_norm(x, g):
    xf = x.astype(jnp.float32)
    y = xf * lax.rsqrt(jnp.mean(xf * xf, axis=-1, keepdims=True) + EPS)
    return (y * g.astype(jnp.float32)).astype(x.dtype)


def rotary(x, pos, n_rot):
    half = n_rot // 2
    inv = ROPE_THETA ** (-jnp.arange(half, dtype=jnp.float32) * 2.0 / n_rot)
    ang = pos[:, None] * inv[None, :]
    ang = ang.reshape((1, ang.shape[0]) + (1,) * (x.ndim - 3) + (half,))
    cos = jnp.cos(ang).astype(x.dtype)
    sin = jnp.sin(ang).astype(x.dtype)
    x1 = x[..., :half]
    x2 = x[..., half:n_rot]
    return jnp.concatenate([x1 * cos - x2 * sin, x2 * cos + x1 * sin, x[..., n_rot:]], axis=-1)


def mixer_inputs(h, pos, w_in, g_q_a, g_k_a, g_k_idx, g_cq, w_uq, g_ckv, g_q_b):
    n, t, _ = h.shape
    qa, ka, va, qi, ki, wi, cq, ckv, kpe = jnp.split(h @ w_in, IN_OFFSETS, axis=-1)
    qa = rotary(rms_norm(qa.reshape(n, t, H_A, DH_A), g_q_a), pos, ROT_A)
    ka = rotary(rms_norm(ka.reshape(n, t, HKV_A, DH_A), g_k_a), pos, ROT_A)
    va = va.reshape(n, t, HKV_A, DH_A)
    qi = rotary(qi.reshape(n, t, H_IDX, D_IDX), pos, ROT_IDX)
    ki = rotary(rms_norm(ki, g_k_idx), pos, ROT_IDX)
    wi = wi * (H_IDX * D_IDX) ** -0.5
    qb = (rms_norm(cq, g_cq) @ w_uq).reshape(n, t, H_B, DN_B + DR_B)
    qb = rms_norm(jnp.concatenate([qb[..., :DN_B], rotary(qb[..., DN_B:], pos, DR_B)], axis=-1), g_q_b)
    ckv = rms_norm(ckv, g_ckv)
    kpe = rotary(kpe, pos, DR_B)
    return qa, ka, va, qi, ki, wi, qb, ckv, kpe


def mla_keys(ckv, kpe, w_uk, w_uv, g_k_b):
    n, s, _ = ckv.shape
    k_nope = jnp.einsum('nsc,chd->nshd', ckv, w_uk.reshape(D_C, H_B, DN_B))
    k_pe = jnp.broadcast_to(kpe[:, :, None, :], (n, s, H_B, DR_B))
    k = rms_norm(jnp.concatenate([k_nope, k_pe], axis=-1), g_k_b)
    v = jnp.einsum('nsc,chd->nshd', ckv, w_uv.reshape(D_C, H_B, DV_B))
    return k, v


def index_scores(qi, wi, ki):
    s = jax.nn.relu(jnp.einsum('nqhd,nld->nqhl', qi, ki).astype(jnp.float32))
    return jnp.einsum('nqh,nqhl->nql', wi.astype(jnp.float32), s)


def attend_selected(q, k_g, v_g, valid):
    n, t = q.shape[:2]
    qg = q.reshape(n, t, HKV_A, H_A // HKV_A, DH_A)
    s = jnp.einsum('nqhgd,nqkhd->nqhgk', qg, k_g).astype(jnp.float32) * DH_A ** -0.5
    s = jnp.where(valid[:, :, None, None, :], s, -jnp.inf)
    p = jax.nn.softmax(s, axis=-1).astype(v_g.dtype)
    return jnp.einsum('nqhgk,nqkhd->nqhgd', p, v_g).reshape(n, t, H_A * DH_A)


def gather_rows(a, i):
    return jax.vmap(lambda ab, ib: ab[ib])(a, i)


def prompt_mixers(h, proj_w, w_uk, w_uv, g_k_b):
    n, s, _ = h.shape
    pos = jnp.arange(s, dtype=jnp.float32)
    qa, ka, va, qi, ki, wi, qb, ckv, kpe = mixer_inputs(h, pos, *proj_w)
    kb, vb = mla_keys(ckv, kpe, w_uk, w_uv, g_k_b)
    n_blk = s // Q_BLOCK
    k_top = min(TOPK_MAX, s // 4)
    key_pos = jnp.arange(s)
    scale_b = (DN_B + DR_B) ** -0.5

    def to_blocks(a):
        return a.reshape((n, n_blk, Q_BLOCK) + a.shape[2:]).swapaxes(0, 1)

    def block(args):
        qa_b, qi_b, wi_b, qb_b, start = args
        qpos = start + jnp.arange(Q_BLOCK)
        causal = key_pos[None, :] <= qpos[:, None]
        sc = jnp.where(causal[None], index_scores(qi_b, wi_b, ki), -jnp.inf)
        _, idx = lax.top_k(sc, k_top)
        out_a = attend_selected(qa_b, gather_rows(ka, idx), gather_rows(va, idx),
                                idx <= qpos[None, :, None])
        sb = jnp.einsum('nqhd,nshd->nhqs', qb_b, kb).astype(jnp.float32) * scale_b
        pb = jax.nn.softmax(jnp.where(causal[None, None], sb, -jnp.inf), axis=-1).astype(vb.dtype)
        out_b = jnp.einsum('nhqs,nshd->nqhd', pb, vb).reshape(n, Q_BLOCK, H_B * DV_B)
        return jnp.concatenate([out_a, out_b], axis=-1)

    starts = jnp.arange(n_blk) * Q_BLOCK
    o = lax.map(block, (to_blocks(qa), to_blocks(qi), to_blocks(wi), to_blocks(qb), starts))
    o = o.swapaxes(0, 1).reshape(n, s, D_MIX)
    return o, ka, va, ki, ckv, kpe


def sample_mixers(h, cache_a_k, cache_a_v, cache_a_kidx, cache_b_ckv, cache_b_kpe, page_table,
                  proj_w, w_uk, w_uv, g_k_b):
    n, t, _ = h.shape
    n_pages = PAST_LEN // PAGE_SIZE
    L = PAST_LEN + t
    pos = PAST_LEN + jnp.arange(t, dtype=jnp.float32)
    qa, ka, va, qi, ki, wi, qb, ckv, kpe = mixer_inputs(h, pos, *proj_w)
    qpos = PAST_LEN + jnp.arange(t)

    k_top = min(TOPK_MAX, L // 4)
    ki_all = jnp.concatenate([cache_a_kidx[page_table].reshape(n, PAST_LEN, D_IDX), ki], axis=1)
    causal = jnp.arange(L)[None, :] <= qpos[:, None]
    sc = jnp.where(causal[None], index_scores(qi, wi, ki_all), -jnp.inf)
    _, idx = lax.top_k(sc, k_top)
    is_past = (idx < PAST_LEN)[..., None, None]
    phys = jax.vmap(lambda pt, lp: pt[lp])(page_table, jnp.minimum(idx // PAGE_SIZE, n_pages - 1))
    off = idx % PAGE_SIZE
    i_new = jnp.clip(idx - PAST_LEN, 0, t - 1)
    k_g = jnp.where(is_past, cache_a_k[phys, off], gather_rows(ka, i_new))
    v_g = jnp.where(is_past, cache_a_v[phys, off], gather_rows(va, i_new))
    out_a = attend_selected(qa, k_g, v_g, idx <= qpos[None, :, None])

    scale_b = (DN_B + DR_B) ** -0.5

    def page_part(phys_pg):
        k, v = mla_keys(cache_b_ckv[phys_pg], cache_b_kpe[phys_pg], w_uk, w_uv, g_k_b)
        s = jnp.einsum('nqhd,nshd->nhqs', qb, k).astype(jnp.float32) * scale_b
        m = s.max(axis=-1)
        p = jnp.exp(s - m[..., None])
        return m, p.sum(axis=-1), jnp.einsum('nhqs,nshd->nhqd', p, v.astype(jnp.float32))

    m_p, l_p, a_p = lax.map(page_part, page_table.T)
    kb_new, vb_new = mla_keys(ckv, kpe, w_uk, w_uv, g_k_b)
    s_n = jnp.einsum('nqhd,nshd->nhqs', qb, kb_new).astype(jnp.float32) * scale_b
    s_n = jnp.where(jnp.tril(jnp.ones((t, t), dtype=bool))[None, None], s_n, -jnp.inf)
    m_n = s_n.max(axis=-1)
    p_n = jnp.exp(s_n - m_n[..., None])
    l_n = p_n.sum(axis=-1)
    a_n = jnp.einsum('nhqs,nshd->nhqd', p_n, vb_new.astype(jnp.float32))
    m = jnp.maximum(m_p.max(axis=0), m_n)
    c_p = jnp.exp(m_p - m)
    c_n = jnp.exp(m_n - m)
    l_tot = (c_p * l_p).sum(axis=0) + c_n * l_n
    acc = (c_p[..., None] * a_p).sum(axis=0) + c_n[..., None] * a_n
    out_b = (acc / l_tot[..., None]).astype(h.dtype).transpose(0, 2, 1, 3).reshape(n, t, H_B * DV_B)
    return jnp.concatenate([out_a, out_b], axis=-1), ka, va, ki, ckv, kpe


def peer_ffn(h2d, w_peer_q, peer_sub_keys, peer_u, peer_v):
    t = h2d.shape[0]
    nb = -(-t // PEER_BLOCK)
    hp = jnp.pad(h2d, ((0, nb * PEER_BLOCK - t), (0, 0))).reshape(nb, PEER_BLOCK, D_MODEL)

    def block(xb):
        q = (xb @ w_peer_q).reshape(PEER_BLOCK, PEER_HEADS, 2, D_KEY // 2)
        s = jnp.einsum('thpd,hpnd->thpn', q, peer_sub_keys).astype(jnp.float32)
        sv, si = lax.top_k(s, PEER_TOPK)
        cand = (sv[:, :, 0, :, None] + sv[:, :, 1, None, :]).reshape(PEER_BLOCK, PEER_HEADS, PEER_TOPK * PEER_TOPK)
        cv, ci = lax.top_k(cand, PEER_TOPK)
        e = (jnp.take_along_axis(si[:, :, 0], ci // PEER_TOPK, axis=-1) * N_KEYS
             + jnp.take_along_axis(si[:, :, 1], ci % PEER_TOPK, axis=-1))
        g = jax.nn.softmax(cv, axis=-1)
        act = jax.nn.gelu(jnp.einsum('td,thkd->thk', xb, peer_u[e]).astype(jnp.float32))
        return jnp.einsum('thk,thkd->td', (g * act).astype(xb.dtype), peer_v[e])

    return lax.map(block, hp).reshape(nb * PEER_BLOCK, D_MODEL)[:t]


def ffn_residual(x1, g_norm_ffn, w_peer_q, peer_sub_keys, peer_u, peer_v):
    h = rms_norm(x1, g_norm_ffn).reshape(-1, D_MODEL)
    return x1 + peer_ffn(h, w_peer_q, peer_sub_keys, peer_u, peer_v).reshape(x1.shape)


def setup_inputs(seed: int = 0) -> dict:
    key = jax.random.key(seed)
    ks = jax.random.split(key, 32)
    f32 = jnp.float32
    n_pages = PAST_LEN // PAGE_SIZE
    n_used = DEC_BATCH * n_pages
    n_phys = n_used + max(1, n_used // 4)

    def nrm(k, shape, scale=1.0):
        return jax.random.normal(k, shape, f32) * scale

    def gain(k, d):
        return 1.0 + 0.05 * jax.random.normal(k, (d,), f32)

    page_table = jax.random.permutation(ks[7], n_phys)[:n_used].reshape(DEC_BATCH, n_pages).astype(jnp.int32)
    return {
        'x_prompt': nrm(ks[0], (BATCH, SEQ, D_MODEL)),
        'x_sample': nrm(ks[1], (DEC_BATCH, DEC_SEQ, D_MODEL)),
        'cache_a_k': nrm(ks[2], (n_phys, PAGE_SIZE, HKV_A, DH_A)),
        'cache_a_v': nrm(ks[3], (n_phys, PAGE_SIZE, HKV_A, DH_A)),
        'cache_a_kidx': nrm(ks[4], (n_phys, PAGE_SIZE, D_IDX)),
        'cache_b_ckv': nrm(ks[5], (n_phys, PAGE_SIZE, D_C)),
        'cache_b_kpe': nrm(ks[6], (n_phys, PAGE_SIZE, DR_B)),
        'page_table': page_table,
        'g_norm_mix': gain(ks[8], D_MODEL),
        'w_in': nrm(ks[9], (D_MODEL, D_IN), D_MODEL ** -0.5),
        'g_q_a': gain(ks[10], DH_A),
        'g_k_a': gain(ks[11], DH_A),
        'g_k_idx': gain(ks[12], D_IDX),
        'g_cq': gain(ks[13], D_CQ),
        'w_uq': nrm(ks[14], (D_CQ, H_B * (DN_B + DR_B)), D_CQ ** -0.5),
        'g_ckv': gain(ks[15], D_C),
        'w_uk': nrm(ks[16], (D_C, H_B * DN_B), D_C ** -0.5),
        'w_uv': nrm(ks[17], (D_C, H_B * DV_B), D_C ** -0.5),
        'g_q_b': gain(ks[18], DN_B + DR_B),
        'g_k_b': gain(ks[19], DN_B + DR_B),
        'w_out': nrm(ks[20], (D_MIX, D_MODEL), D_MIX ** -0.5),
        'g_norm_ffn': gain(ks[21], D_MODEL),
        'w_peer_q': nrm(ks[22], (D_MODEL, PEER_HEADS * D_KEY), D_MODEL ** -0.5),
        'peer_sub_keys': nrm(ks[23], (PEER_HEADS, 2, N_KEYS, D_KEY // 2), (D_KEY // 2) ** -0.5),
        'peer_u': nrm(ks[24], (N_EXPERTS, D_MODEL), D_MODEL ** -0.5),
        'peer_v': nrm(ks[25], (N_EXPERTS, D_MODEL), PEER_HEADS ** -0.5),
    }


def reference(x_prompt, x_sample, cache_a_k, cache_a_v, cache_a_kidx, cache_b_ckv, cache_b_kpe, page_table,
              g_norm_mix, w_in, g_q_a, g_k_a, g_k_idx, g_cq, w_uq, g_ckv, w_uk, w_uv, g_q_b, g_k_b,
              w_out, g_norm_ffn, w_peer_q, peer_sub_keys, peer_u, peer_v):
    proj_w = (w_in, g_q_a, g_k_a, g_k_idx, g_cq, w_uq, g_ckv, g_q_b)
    yp = x_prompt
    ys = x_sample
    for _ in range(DEPTH):
        o_p, pk, pv, pki, pckv, pkpe = prompt_mixers(rms_norm(yp, g_norm_mix), proj_w, w_uk, w_uv, g_k_b)
        yp = ffn_residual(yp + o_p @ w_out, g_norm_ffn, w_peer_q, peer_sub_keys, peer_u, peer_v)
        o_s, sk, sv, ski, sckv, skpe = sample_mixers(rms_norm(ys, g_norm_mix), cache_a_k, cache_a_v, cache_a_kidx,
                                                     cache_b_ckv, cache_b_kpe, page_table, proj_w, w_uk, w_uv, g_k_b)
        ys = ffn_residual(ys + o_s @ w_out, g_norm_ffn, w_peer_q, peer_sub_keys, peer_u, peer_v)
    return (yp, ys, pk, pv, pki, pckv, pkpe, sk, sv, ski, sckv, skpe)
```

```python
import jax
import jax.numpy as jnp
from jax import lax
from jax.experimental import pallas as pl

D_MODEL = 1024
PAST_LEN = 8192
PAGE_SIZE = 128
ROPE_THETA = 500000.0
EPS = 1e-6
H_A = 8
HKV_A = 2
DH_A = 64
ROT_A = DH_A // 4
H_IDX = 4
D_IDX = 64
ROT_IDX = D_IDX // 4
TOPK_MAX = 256
H_B = 8
DN_B = 64
DR_B = 32
DV_B = 64
D_CQ = 384
D_C = 256
PEER_HEADS = 8
N_KEYS = 128
D_KEY = 128
PEER_TOPK = 16
Q_BLOCK = 128
PEER_BLOCK = 128
IN_WIDTHS = (H_A * DH_A, HKV_A * DH_A, HKV_A * DH_A, H_IDX * D_IDX, D_IDX, H_IDX, D_CQ, D_C, DR_B)
IN_OFFSETS = tuple(sum(IN_WIDTHS[:i + 1]) for i in range(len(IN_WIDTHS) - 1))
D_MIX = H_A * DH_A + H_B * DV_B

NORM_ROWS = 512


def _rms_norm_kernel(x_ref, g_ref, o_ref):
    x = x_ref[...]
    o_ref[...] = x * lax.rsqrt(jnp.mean(x * x, axis=-1, keepdims=True) + EPS) * g_ref[...]


def rms_norm_rows(x2d, g):
    t, d = x2d.shape
    return pl.pallas_call(
        _rms_norm_kernel,
        grid=(t // NORM_ROWS,),
        in_specs=[pl.BlockSpec((NORM_ROWS, d), lambda i: (i, 0)),
                  pl.BlockSpec((1, d), lambda i: (0, 0))],
        out_specs=pl.BlockSpec((NORM_ROWS, d), lambda i: (i, 0)),
        out_shape=jax.ShapeDtypeStruct((t, d), jnp.float32),
        name="rms_norm_rows",
    )(x2d, g.reshape(1, d))


def rms_norm(x, g):
    xf = x.astype(jnp.float32)
    y = xf * lax.rsqrt(jnp.mean(xf * xf, axis=-1, keepdims=True) + EPS)
    return (y * g.astype(jnp.float32)).astype(x.dtype)


def rotary(x, pos, n_rot):
    half = n_rot // 2
    inv = ROPE_THETA ** (-jnp.arange(half, dtype=jnp.float32) * 2.0 / n_rot)
    ang = pos[:, None] * inv[None, :]
    ang = ang.reshape((1, ang.shape[0]) + (1,) * (x.ndim - 3) + (half,))
    cos = jnp.cos(ang).astype(x.dtype)
    sin = jnp.sin(ang).astype(x.dtype)
    x1 = x[..., :half]
    x2 = x[..., half:n_rot]
    return jnp.concatenate([x1 * cos - x2 * sin, x2 * cos + x1 * sin, x[..., n_rot:]], axis=-1)


def mixer_inputs(h, pos, w_in, g_q_a, g_k_a, g_k_idx, g_cq, w_uq, g_ckv, g_q_b):
    n, t, _ = h.shape
    qa, ka, va, qi, ki, wi, cq, ckv, kpe = jnp.split(h @ w_in, IN_OFFSETS, axis=-1)
    qa = rotary(rms_norm(qa.reshape(n, t, H_A, DH_A), g_q_a), pos, ROT_A)
    ka = rotary(rms_norm(ka.reshape(n, t, HKV_A, DH_A), g_k_a), pos, ROT_A)
    va = va.reshape(n, t, HKV_A, DH_A)
    qi = rotary(qi.reshape(n, t, H_IDX, D_IDX), pos, ROT_IDX)
    ki = rotary(rms_norm(ki, g_k_idx), pos, ROT_IDX)
    wi = wi * (H_IDX * D_IDX) ** -0.5
    qb = (rms_norm(cq, g_cq) @ w_uq).reshape(n, t, H_B, DN_B + DR_B)
    qb = rms_norm(jnp.concatenate([qb[..., :DN_B], rotary(qb[..., DN_B:], pos, DR_B)], axis=-1), g_q_b)
    ckv = rms_norm(ckv, g_ckv)
    kpe = rotary(kpe, pos, DR_B)
    return qa, ka, va, qi, ki, wi, qb, ckv, kpe


def mla_keys(ckv, kpe, w_uk, w_uv, g_k_b):
    n, s, _ = ckv.shape
    k_nope = jnp.einsum('nsc,chd->nshd', ckv, w_uk.reshape(D_C, H_B, DN_B))
    k_pe = jnp.broadcast_to(kpe[:, :, None, :], (n, s, H_B, DR_B))
    k = rms_norm(jnp.concatenate([k_nope, k_pe], axis=-1), g_k_b)
    v = jnp.einsum('nsc,chd->nshd', ckv, w_uv.reshape(D_C, H_B, DV_B))
    return k, v


def index_scores(qi, wi, ki):
    s = jax.nn.relu(jnp.einsum('nqhd,nld->nqhl', qi, ki).astype(jnp.float32))
    return jnp.einsum('nqh,nqhl->nql', wi.astype(jnp.float32), s)


def attend_selected(q, k_g, v_g, valid):
    n, t = q.shape[:2]
    qg = q.reshape(n, t, HKV_A, H_A // HKV_A, DH_A)
    s = jnp.einsum('nqhgd,nqkhd->nqhgk', qg, k_g).astype(jnp.float32) * DH_A ** -0.5
    s = jnp.where(valid[:, :, None, None, :], s, -jnp.inf)
    p = jax.nn.softmax(s, axis=-1).astype(v_g.dtype)
    return jnp.einsum('nqhgk,nqkhd->nqhgd', p, v_g).reshape(n, t, H_A * DH_A)


def gather_rows(a, i):
    return jax.vmap(lambda ab, ib: ab[ib])(a, i)


def prompt_mixers(h, proj_w, w_uk, w_uv, g_k_b):
    n, s, _ = h.shape
    pos = jnp.arange(s, dtype=jnp.float32)
    qa, ka, va, qi, ki, wi, qb, ckv, kpe = mixer_inputs(h, pos, *proj_w)
    kb, vb = mla_keys(ckv, kpe, w_uk, w_uv, g_k_b)
    n_blk = s // Q_BLOCK
    k_top = min(TOPK_MAX, s // 4)
    key_pos = jnp.arange(s)
    scale_b = (DN_B + DR_B) ** -0.5

    def to_blocks(a):
        return a.reshape((n, n_blk, Q_BLOCK) + a.shape[2:]).swapaxes(0, 1)

    def block(args):
        qa_b, qi_b, wi_b, qb_b, start = args
        qpos = start + jnp.arange(Q_BLOCK)
        causal = key_pos[None, :] <= qpos[:, None]
        sc = jnp.where(causal[None], index_scores(qi_b, wi_b, ki), -jnp.inf)
        _, idx = lax.top_k(sc, k_top)
        out_a = attend_selected(qa_b, gather_rows(ka, idx), gather_rows(va, idx),
                                idx <= qpos[None, :, None])
        sb = jnp.einsum('nqhd,nshd->nhqs', qb_b, kb).astype(jnp.float32) * scale_b
        pb = jax.nn.softmax(jnp.where(causal[None, None], sb, -jnp.inf), axis=-1).astype(vb.dtype)
        out_b = jnp.einsum('nhqs,nshd->nqhd', pb, vb).reshape(n, Q_BLOCK, H_B * DV_B)
        return jnp.concatenate([out_a, out_b], axis=-1)

    starts = jnp.arange(n_blk) * Q_BLOCK
    o = lax.map(block, (to_blocks(qa), to_blocks(qi), to_blocks(wi), to_blocks(qb), starts))
    o = o.swapaxes(0, 1).reshape(n, s, D_MIX)
    return o, ka, va, ki, ckv, kpe


def sample_mixers(h, cache_a_k, cache_a_v, cache_a_kidx, cache_b_ckv, cache_b_kpe, page_table,
                  proj_w, w_uk, w_uv, g_k_b):
    n, t, _ = h.shape
    n_pages = PAST_LEN // PAGE_SIZE
    L = PAST_LEN + t
    pos = PAST_LEN + jnp.arange(t, dtype=jnp.float32)
    qa, ka, va, qi, ki, wi, qb, ckv, kpe = mixer_inputs(h, pos, *proj_w)
    qpos = PAST_LEN + jnp.arange(t)

    k_top = min(TOPK_MAX, L // 4)
    ki_all = jnp.concatenate([cache_a_kidx[page_table].reshape(n, PAST_LEN, D_IDX), ki], axis=1)
    causal = jnp.arange(L)[None, :] <= qpos[:, None]
    sc = jnp.where(causal[None], index_scores(qi, wi, ki_all), -jnp.inf)
    _, idx = lax.top_k(sc, k_top)
    is_past = (idx < PAST_LEN)[..., None, None]
    phys = jax.vmap(lambda pt, lp: pt[lp])(page_table, jnp.minimum(idx // PAGE_SIZE, n_pages - 1))
    off = idx % PAGE_SIZE
    i_new = jnp.clip(idx - PAST_LEN, 0, t - 1)
    k_g = jnp.where(is_past, cache_a_k[phys, off], gather_rows(ka, i_new))
    v_g = jnp.where(is_past, cache_a_v[phys, off], gather_rows(va, i_new))
    out_a = attend_selected(qa, k_g, v_g, idx <= qpos[None, :, None])

    scale_b = (DN_B + DR_B) ** -0.5

    def page_part(phys_pg):
        k, v = mla_keys(cache_b_ckv[phys_pg], cache_b_kpe[phys_pg], w_uk, w_uv, g_k_b)
        s = jnp.einsum('nqhd,nshd->nhqs', qb, k).astype(jnp.float32) * scale_b
        m = s.max(axis=-1)
        p = jnp.exp(s - m[..., None])
        return m, p.sum(axis=-1), jnp.einsum('nhqs,nshd->nhqd', p, v.astype(jnp.float32))

    m_p, l_p, a_p = lax.map(page_part, page_table.T)
    kb_new, vb_new = mla_keys(ckv, kpe, w_uk, w_uv, g_k_b)
    s_n = jnp.einsum('nqhd,nshd->nhqs', qb, kb_new).astype(jnp.float32) * scale_b
    s_n = jnp.where(jnp.tril(jnp.ones((t, t), dtype=bool))[None, None], s_n, -jnp.inf)
    m_n = s_n.max(axis=-1)
    p_n = jnp.exp(s_n - m_n[..., None])
    l_n = p_n.sum(axis=-1)
    a_n = jnp.einsum('nhqs,nshd->nhqd', p_n, vb_new.astype(jnp.float32))
    m = jnp.maximum(m_p.max(axis=0), m_n)
    c_p = jnp.exp(m_p - m)
    c_n = jnp.exp(m_n - m)
    l_tot = (c_p * l_p).sum(axis=0) + c_n * l_n
    acc = (c_p[..., None] * a_p).sum(axis=0) + c_n[..., None] * a_n
    out_b = (acc / l_tot[..., None]).astype(h.dtype).transpose(0, 2, 1, 3).reshape(n, t, H_B * DV_B)
    return jnp.concatenate([out_a, out_b], axis=-1), ka, va, ki, ckv, kpe


def peer_ffn(h2d, w_peer_q, peer_sub_keys, peer_u, peer_v):
    t = h2d.shape[0]
    nb = -(-t // PEER_BLOCK)
    hp = jnp.pad(h2d, ((0, nb * PEER_BLOCK - t), (0, 0))).reshape(nb, PEER_BLOCK, D_MODEL)

    def block(xb):
        q = (xb @ w_peer_q).reshape(PEER_BLOCK, PEER_HEADS, 2, D_KEY // 2)
        s = jnp.einsum('thpd,hpnd->thpn', q, peer_sub_keys).astype(jnp.float32)
        sv, si = lax.top_k(s, PEER_TOPK)
        cand = (sv[:, :, 0, :, None] + sv[:, :, 1, None, :]).reshape(PEER_BLOCK, PEER_HEADS, PEER_TOPK * PEER_TOPK)
        cv, ci = lax.top_k(cand, PEER_TOPK)
        e = (jnp.take_along_axis(si[:, :, 0], ci // PEER_TOPK, axis=-1) * N_KEYS
             + jnp.take_along_axis(si[:, :, 1], ci % PEER_TOPK, axis=-1))
        g = jax.nn.softmax(cv, axis=-1)
        act = jax.nn.gelu(jnp.einsum('td,thkd->thk', xb, peer_u[e]).astype(jnp.float32))
        return jnp.einsum('thk,thkd->td', (g * act).astype(xb.dtype), peer_v[e])

    return lax.map(block, hp).reshape(nb * PEER_BLOCK, D_MODEL)[:t]


def ffn_residual(x1, g_norm_ffn, w_peer_q, peer_sub_keys, peer_u, peer_v):
    h = rms_norm(x1, g_norm_ffn).reshape(-1, D_MODEL)
    return x1 + peer_ffn(h, w_peer_q, peer_sub_keys, peer_u, peer_v).reshape(x1.shape)


def kernel(x_prompt, x_sample, cache_a_k, cache_a_v, cache_a_kidx, cache_b_ckv, cache_b_kpe, page_table,
           g_norm_mix, w_in, g_q_a, g_k_a, g_k_idx, g_cq, w_uq, g_ckv, w_uk, w_uv, g_q_b, g_k_b,
           w_out, g_norm_ffn, w_peer_q, peer_sub_keys, peer_u, peer_v):
    proj_w = (w_in, g_q_a, g_k_a, g_k_idx, g_cq, w_uq, g_ckv, g_q_b)
    hp = rms_norm_rows(x_prompt.reshape(-1, D_MODEL), g_norm_mix).reshape(x_prompt.shape)
    hs = rms_norm_rows(x_sample.reshape(-1, D_MODEL), g_norm_mix).reshape(x_sample.shape)
    o_p, pk, pv, pki, pckv, pkpe = prompt_mixers(hp, proj_w, w_uk, w_uv, g_k_b)
    yp = ffn_residual(x_prompt + o_p @ w_out, g_norm_ffn, w_peer_q, peer_sub_keys, peer_u, peer_v)
    o_s, sk, sv, ski, sckv, skpe = sample_mixers(hs, cache_a_k, cache_a_v, cache_a_kidx,
                                                 cache_b_ckv, cache_b_kpe, page_table, proj_w, w_uk, w_uv, g_k_b)
    ys = ffn_residual(x_sample + o_s @ w_out, g_norm_ffn, w_peer_q, peer_sub_keys, peer_u, peer_v)
    return (yp, ys, pk, pv, pki, pckv, pkpe, sk, sv, ski, sckv, skpe)
```

```python
import functools

import jax
import jax.numpy as jnp
from jax import lax
from jax.experimental import pallas as pl
from jax.experimental.pallas import tpu as pltpu

D_MODEL = 1024
PAST_LEN = 8192
PAGE_SIZE = 128
N_PAGES = PAST_LEN // PAGE_SIZE
ROPE_THETA = 500000.0
EPS = 1e-6
H_A = 8
HKV_A = 2
DH_A = 64
ROT_A = DH_A // 4
H_IDX = 4
D_IDX = 64
TOPK_MAX = 256
H_B = 8
DN_B = 64
DR_B = 32
DV_B = 64
D_CQ = 384
D_C = 256
PEER_HEADS = 8
N_KEYS = 128
D_KEY = 128
PEER_TOPK = 16
IN_WIDTHS = (H_A * DH_A, HKV_A * DH_A, HKV_A * DH_A, H_IDX * D_IDX, D_IDX, H_IDX, D_CQ, D_C, DR_B)
IN_OFFSETS = tuple(sum(IN_WIDTHS[:i]) for i in range(len(IN_WIDTHS)))
D_MIX = H_A * DH_A + H_B * DV_B

MM_DTYPE = jnp.bfloat16
F32 = jnp.float32
NEG_INF = float("-inf")
INT_MIN = -2 ** 31
LANES = 128
VMEM_LIMIT = 48 * 1024 * 1024
SAMPLE_VMEM_LIMIT = 56 * 1024 * 1024
NT_DIMS = (((1,), (1,)), ((), ()))

C_QA, C_KA, C_VA, C_QI, C_KIWI, C_CQ, C_CKV, C_KPE, W_IN = 0, 512, 640, 768, 1024, 1152, 1536, 1792, 1920
WI_LANE = 80
PE_LANE = DN_B
DQK_B = DN_B + DR_B
HEAD_PAD = LANES - DQK_B

IN_TILE = 256
ROUTE_TILE = 256
PEER_TOKEN_TILE = 512
PEER_EXPERT_TILE = 1024
Q_BLOCK = 128
B_CHUNK = 1024


def _seg_sum(sq, ind_ref):
    hi = sq.astype(MM_DTYPE)
    lo = (sq - hi.astype(F32)).astype(MM_DTYPE)
    ind = ind_ref[...]
    return jnp.dot(hi, ind, preferred_element_type=F32) + jnp.dot(lo, ind, preferred_element_type=F32)


def _rot(x, c, sm, sp, half):
    return x * c + pltpu.roll(x, LANES - half, 1) * sm + pltpu.roll(x, half, 1) * sp


def _in_proj_kernel(x_ref, rot_ref, gmix_ref, win_ref, ind_ref, indki_ref, g64_ref, gkiwi_ref, gcq_ref, wuq_ref,
                    gqb_ref, gckv_ref, wuk_ref, wuv_ref, gkb_ref,
                    qa_ref, ka_ref, va_ref, kiwi_ref, ckv_ref, kpe_ref,
                    kab_ref, vab_ref, kib_ref, qi_ref, qb_ref, kb_ref, vb_ref):
    x = x_ref[...]
    h = x * lax.rsqrt(jnp.mean(x * x, axis=-1, keepdims=True) + EPS) * gmix_ref[...]
    proj = jnp.dot(h.astype(MM_DTYPE), win_ref[...], preferred_element_type=F32)
    c16, sm16, sp16 = rot_ref[0], rot_ref[1], rot_ref[2]
    c32, sm32, sp32 = rot_ref[3], rot_ref[4], rot_ref[5]
    half16, half32 = ROT_A // 2, DR_B // 2

    def norm_rot64(xc, g):
        rs = lax.rsqrt(_seg_sum(xc * xc, ind_ref) * (1.0 / DH_A) + EPS)
        return _rot(xc * rs * g, c16, sm16, sp16, half16)

    for j in range(H_A * DH_A // LANES):
        xc = proj[:, C_QA + j * LANES:C_QA + (j + 1) * LANES]
        qa_ref[:, j * LANES:(j + 1) * LANES] = norm_rot64(xc, g64_ref[0:1, :]).astype(qa_ref.dtype)
    ka = norm_rot64(proj[:, C_KA:C_KA + LANES], g64_ref[1:2, :])
    ka_ref[...] = ka
    kab_ref[...] = ka.astype(kab_ref.dtype)
    va = proj[:, C_VA:C_VA + LANES]
    va_ref[...] = va
    vab_ref[...] = va.astype(vab_ref.dtype)
    for j in range(H_IDX * D_IDX // LANES):
        xc = proj[:, C_QI + j * LANES:C_QI + (j + 1) * LANES]
        qi_ref[:, j * LANES:(j + 1) * LANES] = _rot(xc, c16, sm16, sp16, half16).astype(qi_ref.dtype)
    xc = proj[:, C_KIWI:C_KIWI + LANES]
    lane = lax.broadcasted_iota(jnp.int32, xc.shape, 1)
    rs = lax.rsqrt(_seg_sum(xc * xc, indki_ref) * (1.0 / D_IDX) + EPS)
    kiwi = _rot(xc * jnp.where(lane < D_IDX, rs, 1.0) * gkiwi_ref[...], c16, sm16, sp16, half16)
    kiwi_ref[...] = kiwi
    kib_ref[...] = kiwi[:, :D_IDX].astype(kib_ref.dtype)
    cq = proj[:, C_CQ:C_CQ + D_CQ]
    cqn = cq * lax.rsqrt(jnp.mean(cq * cq, axis=-1, keepdims=True) + EPS) * gcq_ref[...]
    qb = jnp.dot(cqn.astype(MM_DTYPE), wuq_ref[...], preferred_element_type=F32)
    ckv = proj[:, C_CKV:C_CKV + D_C]
    ckvn = ckv * lax.rsqrt(jnp.mean(ckv * ckv, axis=-1, keepdims=True) + EPS) * gckv_ref[...]
    ckv_ref[...] = ckvn
    kpe = _rot(proj[:, C_KPE:C_KPE + LANES], c32, sm32, sp32, half32)
    kpe_ref[...] = kpe
    ckvb = ckvn.astype(MM_DTYPE)
    knope = jnp.dot(ckvb, wuk_ref[...], preferred_element_type=F32)
    vb_ref[...] = jnp.dot(ckvb, wuv_ref[...], preferred_element_type=F32).astype(vb_ref.dtype)
    for hd in range(H_B):
        sl = slice(hd * LANES, (hd + 1) * LANES)
        qh = _rot(qb[:, sl], c32, sm32, sp32, half32)
        qh = qh * lax.rsqrt(jnp.sum(qh * qh, axis=-1, keepdims=True) * (1.0 / DQK_B) + EPS) * gqb_ref[...]
        qb_ref[:, sl] = qh.astype(qb_ref.dtype)
        kh = knope[:, sl] + kpe
        kh = kh * lax.rsqrt(jnp.sum(kh * kh, axis=-1, keepdims=True) * (1.0 / DQK_B) + EPS) * gkb_ref[...]
        kb_ref[:, sl] = kh.astype(kb_ref.dtype)


def rotary_tables(pos):
    p = pos.shape[0]

    def ang(n_rot):
        inv = ROPE_THETA ** (-jnp.arange(n_rot // 2, dtype=F32) * 2.0 / n_rot)
        return pos[:, None] * inv[None, :]

    a16 = ang(ROT_A)
    cos16, sin16 = jnp.cos(a16), jnp.sin(a16)
    one48 = jnp.ones((p, DH_A - ROT_A), F32)
    zero48 = jnp.zeros((p, DH_A - ROT_A), F32)
    zero8 = jnp.zeros((p, ROT_A // 2), F32)
    c16 = jnp.tile(jnp.concatenate([cos16, cos16, one48], axis=1), (1, 2))
    sm16 = jnp.tile(jnp.concatenate([-sin16, zero8, zero48], axis=1), (1, 2))
    sp16 = jnp.tile(jnp.concatenate([zero8, sin16, zero48], axis=1), (1, 2))
    a32 = ang(DR_B)
    cos32, sin32 = jnp.cos(a32), jnp.sin(a32)
    one64, zero64 = jnp.ones((p, PE_LANE), F32), jnp.zeros((p, PE_LANE), F32)
    one32, zero32 = jnp.ones((p, HEAD_PAD), F32), jnp.zeros((p, HEAD_PAD), F32)
    zero16 = jnp.zeros((p, DR_B // 2), F32)
    c32 = jnp.concatenate([one64, cos32, cos32, one32], axis=1)
    sm32 = jnp.concatenate([zero64, -sin32, zero16, zero32], axis=1)
    sp32 = jnp.concatenate([zero64, zero16, sin32, zero32], axis=1)
    return jnp.stack([c16, sm16, sp16, c32, sm32, sp32], axis=0)


def in_proj_prepare(g_norm_mix, w_in, g_q_a, g_k_a, g_k_idx, g_cq, w_uq, g_ckv, w_uk, w_uv, g_q_b, g_k_b):
    def cols(i, left=0, right=0):
        return jnp.pad(w_in[:, IN_OFFSETS[i]:IN_OFFSETS[i] + IN_WIDTHS[i]], ((0, 0), (left, right)))

    win = jnp.concatenate([cols(0), cols(1), cols(2), cols(3), cols(4, 0, WI_LANE - D_IDX),
                           cols(5, 0, LANES - WI_LANE - H_IDX), cols(6), cols(7),
                           cols(8, PE_LANE, HEAD_PAD)], axis=1).astype(MM_DTYPE)
    lane = jnp.arange(LANES)
    ind = (lane[:, None] // DH_A == lane[None, :] // DH_A).astype(MM_DTYPE)
    indki = ((lane[:, None] < D_IDX) & (lane[None, :] < D_IDX)).astype(MM_DTYPE)
    g64 = jnp.stack([jnp.tile(g_q_a, 2), jnp.tile(g_k_a, 2)], axis=0)
    gkiwi = jnp.concatenate([g_k_idx, jnp.zeros((WI_LANE - D_IDX,), F32),
                             jnp.full((H_IDX,), (H_IDX * D_IDX) ** -0.5, F32),
                             jnp.zeros((LANES - WI_LANE - H_IDX,), F32)]).reshape(1, LANES)
    wuq = jnp.pad(w_uq.reshape(D_CQ, H_B, DQK_B), ((0, 0), (0, 0), (0, HEAD_PAD))).reshape(D_CQ, H_B * LANES)
    wuk = jnp.pad(w_uk.reshape(D_C, H_B, DN_B), ((0, 0), (0, 0), (0, LANES - DN_B))).reshape(D_C, H_B * LANES)
    gqb = jnp.pad(g_q_b, (0, HEAD_PAD)).reshape(1, LANES)
    gkb = jnp.pad(g_k_b, (0, HEAD_PAD)).reshape(1, LANES)
    return (g_norm_mix.reshape(1, D_MODEL), win, ind, indki, g64, gkiwi, g_cq.reshape(1, D_CQ),
            wuq.astype(MM_DTYPE), gqb, g_ckv.reshape(1, D_C), wuk.astype(MM_DTYPE), w_uv.astype(MM_DTYPE), gkb)


def in_proj(x2d, rot, prepared, *, tile):
    t = x2d.shape[0]
    n_rot_blocks = rot.shape[1] // tile
    row = lambda i: (i, 0)
    full = lambda a: pl.BlockSpec(a.shape, lambda i: (0,) * a.ndim)
    widths = [(H_A * DH_A, MM_DTYPE), (LANES, F32), (LANES, F32), (LANES, F32), (D_C, F32), (LANES, F32),
              (LANES, MM_DTYPE), (LANES, MM_DTYPE), (D_IDX, MM_DTYPE), (H_IDX * D_IDX, MM_DTYPE),
              (H_B * LANES, MM_DTYPE), (H_B * LANES, MM_DTYPE), (H_B * DV_B, MM_DTYPE)]
    return pl.pallas_call(
        _in_proj_kernel,
        grid=(t // tile,),
        in_specs=[pl.BlockSpec((tile, D_MODEL), row),
                  pl.BlockSpec((6, tile, LANES), lambda i: (0, i % n_rot_blocks, 0))] + [full(a) for a in prepared],
        out_specs=[pl.BlockSpec((tile, w), row) for w, _ in widths],
        out_shape=[jax.ShapeDtypeStruct((t, w), d) for w, d in widths],
        compiler_params=pltpu.CompilerParams(dimension_semantics=("parallel",), vmem_limit_bytes=VMEM_LIMIT),
        name="in_proj",
    )(x2d, rot, *prepared)


def _sortable_key(x):
    b = pltpu.bitcast(x, jnp.int32)
    return jnp.where(b < 0, b ^ 0x7FFFFFFF, b)


def _count(mask):
    return jnp.sum(jnp.where(mask, 1.0, 0.0), axis=-1, keepdims=True)


def _kth_largest_key(key, k):
    kf = float(k)
    t0 = jnp.where(_count(key >= 0) >= kf, 0, INT_MIN).astype(jnp.int32)

    def body(i, t):
        cand = t + jnp.left_shift(jnp.int32(1), 30 - i)
        return jnp.where(_count(key >= cand) >= kf, cand, t)

    return lax.fori_loop(0, 31, body, t0)


def _topk_mask(score, k, tri_ref):
    key = _sortable_key(score)
    t = _kth_largest_key(key, k)
    gt = key > t
    eq = key == t
    need = float(k) - _count(gt)
    carry = jnp.zeros_like(need)
    parts = []
    for c in range(score.shape[1] // LANES):
        sl = slice(c * LANES, (c + 1) * LANES)
        eq_c = jnp.where(eq[:, sl], 1.0, 0.0)
        incl = jnp.dot(eq_c.astype(MM_DTYPE), tri_ref[...], preferred_element_type=F32)
        rank = incl - eq_c + carry
        parts.append(gt[:, sl] | (eq[:, sl] & (rank < need)))
        carry = carry + jnp.sum(eq_c, axis=-1, keepdims=True)
    return jnp.concatenate(parts, axis=1)


def _tie_rank_matrix():
    lane = jnp.arange(LANES)
    return (lane[:, None] <= lane[None, :]).astype(MM_DTYPE)


def _masked_attend(q, k, v, mask, scale):
    s = lax.dot_general(q, k, NT_DIMS, preferred_element_type=F32) * scale
    s = jnp.where(mask, s, NEG_INF)
    p = jnp.exp(s - jnp.max(s, axis=-1, keepdims=True))
    l = jnp.sum(p, axis=-1, keepdims=True)
    return jnp.dot(p.astype(MM_DTYPE), v, preferred_element_type=F32) / l


def _prompt_attn_kernel(qa_ref, qi_ref, kiwi_ref, qb_ref, ki_ref, ka_ref, va_ref, kb_ref, vb_ref, tri_ref, o_ref,
                        *, k_top):
    qn = qa_ref.shape[1]
    s_len = ki_ref.shape[1]
    q0 = pl.program_id(1) * qn
    qpos = q0 + lax.broadcasted_iota(jnp.int32, (qn, s_len), 0)
    kpos = lax.broadcasted_iota(jnp.int32, (qn, s_len), 1)
    causal = kpos <= qpos

    ki = ki_ref[0]
    qi = qi_ref[0]
    kiwi = kiwi_ref[0]
    score = jnp.zeros((qn, s_len), F32)
    for h in range(H_IDX):
        s = lax.dot_general(qi[:, h * D_IDX:(h + 1) * D_IDX], ki, NT_DIMS, preferred_element_type=F32)
        score = score + kiwi[:, WI_LANE + h:WI_LANE + h + 1] * jnp.maximum(s, 0.0)
    score = jnp.where(causal, score, NEG_INF)
    sel = _topk_mask(score, k_top, tri_ref) & causal

    qa = qa_ref[0]
    ka = ka_ref[0]
    va = va_ref[0]
    outs = []
    group = H_A // HKV_A
    for h in range(H_A):
        g = h // group
        outs.append(_masked_attend(qa[:, h * DH_A:(h + 1) * DH_A], ka[:, g * DH_A:(g + 1) * DH_A],
                                   va[:, g * DH_A:(g + 1) * DH_A], sel, DH_A ** -0.5))
    qb = qb_ref[0]
    for h in range(H_B):
        outs.append(_masked_attend(qb[:, h * LANES:(h + 1) * LANES], kb_ref[0, :, h * LANES:(h + 1) * LANES],
                                   vb_ref[0, :, h * DV_B:(h + 1) * DV_B], causal, DQK_B ** -0.5))
    for j in range(len(outs) // 2):
        o_ref[0, :, j * LANES:(j + 1) * LANES] = jnp.concatenate(outs[2 * j:2 * j + 2], axis=1).astype(o_ref.dtype)


def prompt_attention(qa, qi, kiwi, qb, ki, ka, va, kb, vb, *, q_block):
    n, s_len, _ = qa.shape
    k_top = min(TOPK_MAX, s_len // 4)
    tri = _tie_rank_matrix()
    qspec = lambda a: pl.BlockSpec((1, q_block, a.shape[2]), lambda b, i: (b, i, 0))
    kspec = lambda a: pl.BlockSpec((1, s_len, a.shape[2]), lambda b, i: (b, 0, 0))
    return pl.pallas_call(
        functools.partial(_prompt_attn_kernel, k_top=k_top),
        grid=(n, s_len // q_block),
        in_specs=[qspec(qa), qspec(qi), qspec(kiwi), qspec(qb), kspec(ki), kspec(ka), kspec(va), kspec(kb), kspec(vb),
                  pl.BlockSpec(tri.shape, lambda b, i: (0, 0))],
        out_specs=pl.BlockSpec((1, q_block, D_MIX), lambda b, i: (b, i, 0)),
        out_shape=jax.ShapeDtypeStruct((n, s_len, D_MIX), MM_DTYPE),
        compiler_params=pltpu.CompilerParams(dimension_semantics=("parallel", "parallel"),
                                             vmem_limit_bytes=VMEM_LIMIT),
        name="prompt_attention",
    )(qa, qi, kiwi, qb, ki, ka, va, kb, vb, tri)


def _page_copy(pt_ref, n, p, slot, cache, buf, sem):
    phys = pt_ref[n * N_PAGES + p]
    rows = pl.ds(pl.multiple_of(p * PAGE_SIZE, PAGE_SIZE), PAGE_SIZE)
    return pltpu.make_async_copy(cache.at[phys], buf.at[slot, rows], sem.at[slot])


def _start_pages(pt_ref, n, slot, caches, bufs, sems):
    def body(p, carry):
        for cache, buf, sem in zip(caches, bufs, sems):
            _page_copy(pt_ref, n, p, slot, cache, buf, sem).start()
        return carry
    lax.fori_loop(0, N_PAGES, body, 0)


def _wait_pages(pt_ref, n, slot, caches, bufs, sems):
    def body(p, carry):
        for cache, buf, sem in zip(caches, bufs, sems):
            _page_copy(pt_ref, n, p, slot, cache, buf, sem).wait()
        return carry
    lax.fori_loop(0, N_PAGES, body, 0)


def _fetch_sequence_pages(pt_ref, caches, bufs, sems):
    n = pl.program_id(0)
    slot = n % 2

    @pl.when(n == 0)
    def _():
        _start_pages(pt_ref, n, slot, caches, bufs, sems)

    @pl.when(n + 1 < pl.num_programs(0))
    def _():
        _start_pages(pt_ref, n + 1, 1 - slot, caches, bufs, sems)

    _wait_pages(pt_ref, n, slot, caches, bufs, sems)
    return slot


def _pad_rows(a, rows):
    return jnp.concatenate([a, jnp.zeros((rows - a.shape[0], a.shape[1]), a.dtype)], axis=0)


def _new_key_mask(t_rows, reps):
    tok = lax.broadcasted_iota(jnp.int32, (reps * t_rows, LANES), 0) % t_rows
    j = lax.broadcasted_iota(jnp.int32, (reps * t_rows, LANES), 1)
    return j <= tok


def _sample_a_kernel(pt_ref, qa_ref, qi_ref, kiwi_ref, kin_ref, kan_ref, van_ref, tri_ref,
                     ckidx_hbm, ck_hbm, cv_hbm, o_ref, kidx_buf, k_buf, v_buf, sem_i, sem_k, sem_v):
    slot = _fetch_sequence_pages(pt_ref, (ckidx_hbm, ck_hbm, cv_hbm), (kidx_buf, k_buf, v_buf),
                                 (sem_i, sem_k, sem_v))
    t_new = qa_ref.shape[1]
    qi = qi_ref[0]
    kiwi = kiwi_ref[0]
    ki_past = kidx_buf[slot].astype(MM_DTYPE)
    ki_new = _pad_rows(kin_ref[0], LANES)
    score_p = jnp.zeros((t_new, PAST_LEN), F32)
    score_n = jnp.zeros((t_new, LANES), F32)
    for h in range(H_IDX):
        qh = qi[:, h * D_IDX:(h + 1) * D_IDX]
        wh = kiwi[:, WI_LANE + h:WI_LANE + h + 1]
        score_p = score_p + wh * jnp.maximum(lax.dot_general(qh, ki_past, NT_DIMS, preferred_element_type=F32), 0.0)
        score_n = score_n + wh * jnp.maximum(lax.dot_general(qh, ki_new, NT_DIMS, preferred_element_type=F32), 0.0)
    vis_n = _new_key_mask(t_new, 1)
    score = jnp.concatenate([score_p, jnp.where(vis_n, score_n, NEG_INF)], axis=1)
    k_top = min(TOPK_MAX, (PAST_LEN + t_new) // 4)
    visible = jnp.concatenate([jnp.full((t_new, PAST_LEN), True), vis_n], axis=1)
    sel = _topk_mask(score, k_top, tri_ref) & visible

    group = H_A // HKV_A
    sel_g = jnp.concatenate([sel] * group, axis=0)
    qa = qa_ref[0]
    k_all = k_buf[slot]
    v_all = v_buf[slot]
    outs = []
    for g in range(HKV_A):
        gs = slice(g * DH_A, (g + 1) * DH_A)
        q_g = jnp.concatenate([qa[:, (g * group + j) * DH_A:(g * group + j + 1) * DH_A] for j in range(group)], axis=0)
        k_g = k_all[:, gs].astype(MM_DTYPE)
        v_g = v_all[:, gs].astype(MM_DTYPE)
        kn_g = _pad_rows(kan_ref[0][:, gs], LANES)
        vn_g = _pad_rows(van_ref[0][:, gs], LANES)
        s = jnp.concatenate([lax.dot_general(q_g, k_g, NT_DIMS, preferred_element_type=F32),
                             lax.dot_general(q_g, kn_g, NT_DIMS, preferred_element_type=F32)], axis=1) * DH_A ** -0.5
        s = jnp.where(sel_g, s, NEG_INF)
        p = jnp.exp(s - jnp.max(s, axis=-1, keepdims=True))
        l = jnp.sum(p, axis=-1, keepdims=True)
        pb = p.astype(MM_DTYPE)
        o_g = (jnp.dot(pb[:, :PAST_LEN], v_g, preferred_element_type=F32)
               + jnp.dot(pb[:, PAST_LEN:], vn_g, preferred_element_type=F32)) / l
        outs += [o_g[j * t_new:(j + 1) * t_new] for j in range(group)]
    for j in range(len(outs) // 2):
        o_ref[0, :, j * LANES:(j + 1) * LANES] = jnp.concatenate(outs[2 * j:2 * j + 2], axis=1).astype(o_ref.dtype)


def sample_attention_a(page_table, qa, qi, kiwi, ki_new, ka_new, va_new, cache_kidx, cache_k, cache_v):
    n, t_new, _ = qa.shape
    tri = _tie_rank_matrix()
    seq = lambda a: pl.BlockSpec((1, t_new, a.shape[2]), lambda i, pt: (i, 0, 0))
    hbm = pl.BlockSpec(memory_space=pl.ANY)
    return pl.pallas_call(
        _sample_a_kernel,
        grid_spec=pltpu.PrefetchScalarGridSpec(
            num_scalar_prefetch=1,
            grid=(n,),
            in_specs=[seq(qa), seq(qi), seq(kiwi), seq(ki_new), seq(ka_new), seq(va_new),
                      pl.BlockSpec(tri.shape, lambda i, pt: (0, 0)), hbm, hbm, hbm],
            out_specs=pl.BlockSpec((1, t_new, H_A * DH_A), lambda i, pt: (i, 0, 0)),
            scratch_shapes=[pltpu.VMEM((2, PAST_LEN, D_IDX), F32),
                            pltpu.VMEM((2, PAST_LEN, HKV_A * DH_A), F32),
                            pltpu.VMEM((2, PAST_LEN, HKV_A * DH_A), F32),
                            pltpu.SemaphoreType.DMA((2,)), pltpu.SemaphoreType.DMA((2,)),
                            pltpu.SemaphoreType.DMA((2,))]),
        out_shape=jax.ShapeDtypeStruct((n, t_new, H_A * DH_A), MM_DTYPE),
        compiler_params=pltpu.CompilerParams(dimension_semantics=("arbitrary",),
                                             vmem_limit_bytes=SAMPLE_VMEM_LIMIT),
        name="sample_attention_a",
    )(page_table.reshape(-1), qa, qi, kiwi, ki_new, ka_new, va_new, tri,
      cache_kidx, cache_k.reshape(cache_k.shape[0], PAGE_SIZE, HKV_A * DH_A),
      cache_v.reshape(cache_v.shape[0], PAGE_SIZE, HKV_A * DH_A))


def _sample_b_kernel(pt_ref, qb_ref, kbn_ref, ckvn_ref, wuk_ref, wuv_ref, gkb_ref, place_ref,
                     ckv_hbm, kpe_hbm, o_ref, ckv_buf, kpe_buf, sem_c, sem_p):
    slot = _fetch_sequence_pages(pt_ref, (ckv_hbm, kpe_hbm), (ckv_buf, kpe_buf), (sem_c, sem_p))
    t_new = qb_ref.shape[1]
    rows = H_B * t_new
    qb = qb_ref[0]
    scale = DQK_B ** -0.5

    def attend(carry, keys, latent, mask):
        m, l, acc = carry
        s = jnp.concatenate([lax.dot_general(qb[:, h * LANES:(h + 1) * LANES], keys[h], NT_DIMS,
                                             preferred_element_type=F32) for h in range(H_B)], axis=0) * scale
        if mask is not None:
            s = jnp.where(mask, s, NEG_INF)
        m_new = jnp.maximum(m, jnp.max(s, axis=-1, keepdims=True))
        alpha = jnp.exp(m - m_new)
        p = jnp.exp(s - m_new)
        l = alpha * l + jnp.sum(p, axis=-1, keepdims=True)
        acc = alpha * acc + jnp.dot(p.astype(MM_DTYPE), latent, preferred_element_type=F32)
        return m_new, l, acc

    def chunk(c, carry):
        ks = pl.ds(pl.multiple_of(c * B_CHUNK, B_CHUNK), B_CHUNK)
        ckv_c = ckv_buf[slot, ks, :].astype(MM_DTYPE)
        kpe_c = kpe_buf[slot, ks, :]
        hi = kpe_c.astype(MM_DTYPE)
        lo = (kpe_c - hi.astype(F32)).astype(MM_DTYPE)
        kpe_pad = (jnp.dot(hi, place_ref[...], preferred_element_type=F32)
                   + jnp.dot(lo, place_ref[...], preferred_element_type=F32))
        knope = jnp.dot(ckv_c, wuk_ref[...], preferred_element_type=F32)
        keys = []
        for h in range(H_B):
            kh = knope[:, h * LANES:(h + 1) * LANES] + kpe_pad
            kh = kh * lax.rsqrt(jnp.sum(kh * kh, axis=-1, keepdims=True) * (1.0 / DQK_B) + EPS) * gkb_ref[...]
            keys.append(kh.astype(MM_DTYPE))
        return attend(carry, keys, ckv_c, None)

    init = (jnp.full((rows, 1), NEG_INF, F32), jnp.zeros((rows, 1), F32), jnp.zeros((rows, D_C), F32))
    carry = lax.fori_loop(0, PAST_LEN // B_CHUNK, chunk, init)
    kbn = _pad_rows(kbn_ref[0], LANES)
    ckvn = _pad_rows(ckvn_ref[0].astype(MM_DTYPE), LANES)
    m, l, acc = attend(carry, [kbn[:, h * LANES:(h + 1) * LANES] for h in range(H_B)], ckvn,
                       _new_key_mask(t_new, H_B))
    lat = (acc / l).astype(MM_DTYPE)
    outs = [jnp.dot(lat[h * t_new:(h + 1) * t_new], wuv_ref[:, h * DV_B:(h + 1) * DV_B],
                    preferred_element_type=F32) for h in range(H_B)]
    for j in range(H_B // 2):
        o_ref[0, :, j * LANES:(j + 1) * LANES] = jnp.concatenate(outs[2 * j:2 * j + 2], axis=1).astype(o_ref.dtype)


def sample_attention_b(page_table, qb, kb_new, ckv_new, wuk_p, wuv, gkb, cache_ckv, cache_kpe):
    n, t_new, _ = qb.shape
    place = (jnp.arange(DR_B)[:, None] + PE_LANE == jnp.arange(LANES)[None, :]).astype(MM_DTYPE)
    seq = lambda a: pl.BlockSpec((1, t_new, a.shape[2]), lambda i, pt: (i, 0, 0))
    full = lambda a: pl.BlockSpec(a.shape, lambda i, pt: (0,) * a.ndim)
    hbm = pl.BlockSpec(memory_space=pl.ANY)
    return pl.pallas_call(
        _sample_b_kernel,
        grid_spec=pltpu.PrefetchScalarGridSpec(
            num_scalar_prefetch=1,
            grid=(n,),
            in_specs=[seq(qb), seq(kb_new), seq(ckv_new), full(wuk_p), full(wuv), full(gkb), full(place), hbm, hbm],
            out_specs=pl.BlockSpec((1, t_new, H_B * DV_B), lambda i, pt: (i, 0, 0)),
            scratch_shapes=[pltpu.VMEM((2, PAST_LEN, D_C), F32),
                            pltpu.VMEM((2, PAST_LEN, DR_B), F32),
                            pltpu.SemaphoreType.DMA((2,)), pltpu.SemaphoreType.DMA((2,))]),
        out_shape=jax.ShapeDtypeStruct((n, t_new, H_B * DV_B), MM_DTYPE),
        compiler_params=pltpu.CompilerParams(dimension_semantics=("arbitrary",),
                                             vmem_limit_bytes=SAMPLE_VMEM_LIMIT),
        name="sample_attention_b",
    )(page_table.reshape(-1), qb, kb_new, ckv_new, wuk_p, wuv, gkb, place, cache_ckv, cache_kpe)


def _gelu_tanh(x):
    return 0.5 * x * (1.0 + jnp.tanh(0.7978845608028654 * (x + 0.044715 * (x * x * x))))


def _peer_route_kernel(x_ref, o_ref, wout_ref, g_ref, wqT_ref, subk_ref,
                       x1_ref, h2T_ref, sT_ref, thr_ref, c_ref, top_ref):
    x1 = x_ref[...] + jnp.dot(o_ref[...].astype(MM_DTYPE), wout_ref[...], preferred_element_type=F32)
    x1_ref[...] = x1
    h2 = x1 * lax.rsqrt(jnp.mean(x1 * x1, axis=-1, keepdims=True) + EPS) * g_ref[...]
    h2b = h2.astype(MM_DTYPE)
    h2T_ref[...] = h2.T.astype(h2T_ref.dtype)
    qT = lax.dot_general(wqT_ref[...], h2b, NT_DIMS, preferred_element_type=F32)
    half = D_KEY // 2
    for h in range(PEER_HEADS):
        for p in range(2):
            r = (h * 2 + p) * half
            sT = jnp.dot(subk_ref[h * 2 + p], qT[r:r + half, :].astype(MM_DTYPE), preferred_element_type=F32)
            sT_ref[p * PEER_HEADS + h] = sT
            work = sT
            for k in range(PEER_TOPK):
                m = jnp.max(work, axis=0, keepdims=True)
                top_ref[p, k:k + 1, :] = m
                work = jnp.where(work == m, NEG_INF, work)
        v0 = top_ref[0]
        v1 = top_ref[1]
        parts = [v0[0:1, :] + v1]
        for a in range(1, 8):
            parts.append(v0[a:a + 1, :] + v1[0:8, :])
        parts.append(v0[8:16, :] + v1[0:1, :])
        cand = jnp.concatenate(parts, axis=0)
        work = cand
        for k in range(PEER_TOPK):
            m = jnp.max(work, axis=0, keepdims=True)
            work = jnp.where(work == m, NEG_INF, work)
        thr = m
        top = v0[0:1, :] + v1[0:1, :]
        z = jnp.sum(jnp.where(cand >= thr, jnp.exp(cand - top), 0.0), axis=0, keepdims=True)
        thr_ref[h:h + 1, :] = thr
        c_ref[h:h + 1, :] = top + jnp.log(z)


def peer_route(x, o, wout_b, g_ffn, wqT_b, subk_b, *, tile):
    t = x.shape[0]
    row = lambda i: (i, 0)
    col = lambda i: (0, i)
    fixed2 = lambda i: (0, 0)
    return pl.pallas_call(
        _peer_route_kernel,
        grid=(t // tile,),
        in_specs=[pl.BlockSpec((tile, D_MODEL), row),
                  pl.BlockSpec((tile, o.shape[1]), row),
                  pl.BlockSpec(wout_b.shape, fixed2),
                  pl.BlockSpec((1, D_MODEL), fixed2),
                  pl.BlockSpec(wqT_b.shape, fixed2),
                  pl.BlockSpec(subk_b.shape, lambda i: (0, 0, 0))],
        out_specs=[pl.BlockSpec((tile, D_MODEL), row),
                   pl.BlockSpec((D_MODEL, tile), col),
                   pl.BlockSpec((2 * PEER_HEADS, N_KEYS, tile), lambda i: (0, 0, i)),
                   pl.BlockSpec((PEER_HEADS, tile), col),
                   pl.BlockSpec((PEER_HEADS, tile), col)],
        out_shape=[jax.ShapeDtypeStruct((t, D_MODEL), F32),
                   jax.ShapeDtypeStruct((D_MODEL, t), MM_DTYPE),
                   jax.ShapeDtypeStruct((2 * PEER_HEADS, N_KEYS, t), F32),
                   jax.ShapeDtypeStruct((PEER_HEADS, t), F32),
                   jax.ShapeDtypeStruct((PEER_HEADS, t), F32)],
        scratch_shapes=[pltpu.VMEM((2, PEER_TOPK, tile), F32)],
        compiler_params=pltpu.CompilerParams(dimension_semantics=("parallel",), vmem_limit_bytes=VMEM_LIMIT),
        name="peer_route",
    )(x, o, wout_b, g_ffn.reshape(1, D_MODEL), wqT_b, subk_b)


def _peer_dense_kernel(h2T_ref, u_ref, vT_ref, s0_ref, s1_ref, thr_ref, c_ref, x1_ref, y_ref, acc_ref):
    ei = pl.program_id(1)
    rows_per_step = s0_ref.shape[1]

    @pl.when(ei == 0)
    def _():
        acc_ref[...] = jnp.zeros_like(acc_ref)

    hT = jnp.dot(u_ref[...], h2T_ref[...], preferred_element_type=F32)
    ws = []
    for ii in range(rows_per_step):
        hid = hT[ii * N_KEYS:(ii + 1) * N_KEYS, :]
        gate = jnp.zeros_like(hid)
        for h in range(PEER_HEADS):
            logit = s0_ref[h, ii:ii + 1, :] + s1_ref[h]
            gate = gate + jnp.where(logit >= thr_ref[h:h + 1, :], jnp.exp(logit - c_ref[h:h + 1, :]), 0.0)
        ws.append((gate * _gelu_tanh(hid)).astype(MM_DTYPE))
    w = jnp.concatenate(ws, axis=0)
    acc_ref[...] += jnp.dot(vT_ref[...], w, preferred_element_type=F32)

    @pl.when(ei == pl.num_programs(1) - 1)
    def _():
        y_ref[...] = x1_ref[...] + acc_ref[...].T


def peer_dense(h2T, u_b, vT_b, sT, thr, c, x1, *, tile, etile):
    t = x1.shape[0]
    n_exp = u_b.shape[0]
    rows = etile // N_KEYS
    return pl.pallas_call(
        _peer_dense_kernel,
        grid=(t // tile, n_exp // etile),
        in_specs=[pl.BlockSpec((D_MODEL, tile), lambda ti, ei: (0, ti)),
                  pl.BlockSpec((etile, D_MODEL), lambda ti, ei: (ei, 0)),
                  pl.BlockSpec((D_MODEL, etile), lambda ti, ei: (0, ei)),
                  pl.BlockSpec((PEER_HEADS, rows, tile), lambda ti, ei: (0, ei, ti)),
                  pl.BlockSpec((PEER_HEADS, N_KEYS, tile), lambda ti, ei: (1, 0, ti)),
                  pl.BlockSpec((PEER_HEADS, tile), lambda ti, ei: (0, ti)),
                  pl.BlockSpec((PEER_HEADS, tile), lambda ti, ei: (0, ti)),
                  pl.BlockSpec((tile, D_MODEL), lambda ti, ei: (ti, 0))],
        out_specs=pl.BlockSpec((tile, D_MODEL), lambda ti, ei: (ti, 0)),
        out_shape=jax.ShapeDtypeStruct((t, D_MODEL), F32),
        scratch_shapes=[pltpu.VMEM((D_MODEL, tile), F32)],
        compiler_params=pltpu.CompilerParams(dimension_semantics=("parallel", "arbitrary"),
                                             vmem_limit_bytes=VMEM_LIMIT),
        name="peer_dense",
    )(h2T, u_b, vT_b, sT, sT, thr, c, x1)


def peer_prepare(w_out, w_peer_q, peer_sub_keys, peer_u, peer_v):
    return (w_out.astype(MM_DTYPE), w_peer_q.T.astype(MM_DTYPE),
            peer_sub_keys.reshape(2 * PEER_HEADS, N_KEYS, D_KEY // 2).astype(MM_DTYPE),
            peer_u.astype(MM_DTYPE), peer_v.T.astype(MM_DTYPE))


def peer_layer(x2d, o2d, g_ffn, prepared):
    wout_b, wqT_b, subk_b, u_b, vT_b = prepared
    x1, h2T, sT, thr, c = peer_route(x2d, o2d, wout_b, g_ffn, wqT_b, subk_b, tile=ROUTE_TILE)
    return peer_dense(h2T, u_b, vT_b, sT, thr, c, x1, tile=PEER_TOKEN_TILE, etile=PEER_EXPERT_TILE)


def kernel(x_prompt, x_sample, cache_a_k, cache_a_v, cache_a_kidx, cache_b_ckv, cache_b_kpe, page_table,
           g_norm_mix, w_in, g_q_a, g_k_a, g_k_idx, g_cq, w_uq, g_ckv, w_uk, w_uv, g_q_b, g_k_b,
           w_out, g_norm_ffn, w_peer_q, peer_sub_keys, peer_u, peer_v):
    n_p, s_len, _ = x_prompt.shape
    n_s, t_new, _ = x_sample.shape
    in_prep = in_proj_prepare(g_norm_mix, w_in, g_q_a, g_k_a, g_k_idx, g_cq, w_uq, g_ckv, w_uk, w_uv, g_q_b, g_k_b)
    wuk_p, wuv_b, gkb = in_prep[10], in_prep[11], in_prep[12]
    peer_prep = peer_prepare(w_out, w_peer_q, peer_sub_keys, peer_u, peer_v)

    rot_p = rotary_tables(jnp.arange(s_len, dtype=F32))
    (_qa, ka, va, kiwi, ckv, kpe, ka_b, va_b, ki_b, qi, qb, kb, vb) = in_proj(
        x_prompt.reshape(-1, D_MODEL), rot_p, in_prep, tile=IN_TILE)
    seq_p = lambda a: a.reshape(n_p, s_len, a.shape[-1])
    o_p = prompt_attention(seq_p(_qa), seq_p(qi), seq_p(kiwi), seq_p(qb), seq_p(ki_b), seq_p(ka_b), seq_p(va_b),
                           seq_p(kb), seq_p(vb), q_block=Q_BLOCK)
    yp = peer_layer(x_prompt.reshape(-1, D_MODEL), o_p.reshape(-1, D_MIX), g_norm_ffn, peer_prep)
    prompt_rows = (ka.reshape(n_p, s_len, HKV_A, DH_A), va.reshape(n_p, s_len, HKV_A, DH_A),
                   kiwi[:, :D_IDX].reshape(n_p, s_len, D_IDX), ckv.reshape(n_p, s_len, D_C),
                   kpe[:, PE_LANE:PE_LANE + DR_B].reshape(n_p, s_len, DR_B))

    rot_s = rotary_tables(jnp.tile(PAST_LEN + jnp.arange(t_new, dtype=F32), IN_TILE // t_new))
    (_qa, ka, va, kiwi, ckv, kpe, ka_b, va_b, ki_b, qi, qb, kb, vb) = in_proj(
        x_sample.reshape(-1, D_MODEL), rot_s, in_prep, tile=IN_TILE)
    seq_s = lambda a: a.reshape(n_s, t_new, a.shape[-1])
    o_a = sample_attention_a(page_table, seq_s(_qa), seq_s(qi), seq_s(kiwi), seq_s(ki_b), seq_s(ka_b), seq_s(va_b),
                             cache_a_kidx, cache_a_k, cache_a_v)
    o_b = sample_attention_b(page_table, seq_s(qb), seq_s(kb), seq_s(ckv), wuk_p, wuv_b, gkb,
                             cache_b_ckv, cache_b_kpe)
    o_s = jnp.concatenate([o_a, o_b], axis=-1)
    ys = peer_layer(x_sample.reshape(-1, D_MODEL), o_s.reshape(-1, D_MIX), g_norm_ffn, peer_prep)
    sample_rows = (ka.reshape(n_s, t_new, HKV_A, DH_A), va.reshape(n_s, t_new, HKV_A, DH_A),
                   kiwi[:, :D_IDX].reshape(n_s, t_new, D_IDX), ckv.reshape(n_s, t_new, D_C),
                   kpe[:, PE_LANE:PE_LANE + DR_B].reshape(n_s, t_new, DR_B))
    return (yp.reshape(x_prompt.shape), ys.reshape(x_sample.shape)) + prompt_rows + sample_rows
```

```python
import functools

import jax
import jax.numpy as jnp
from jax import lax
from jax.experimental import pallas as pl
from jax.experimental.pallas import tpu as pltpu

D_MODEL = 1024
PAST_LEN = 8192
PAGE_SIZE = 128
N_PAGES = PAST_LEN // PAGE_SIZE
ROPE_THETA = 500000.0
EPS = 1e-6
H_A = 8
HKV_A = 2
DH_A = 64
ROT_A = DH_A // 4
H_IDX = 4
D_IDX = 64
TOPK_MAX = 256
H_B = 8
DN_B = 64
DR_B = 32
DV_B = 64
D_CQ = 384
D_C = 256
PEER_HEADS = 8
N_KEYS = 128
D_KEY = 128
PEER_TOPK = 16
IN_WIDTHS = (H_A * DH_A, HKV_A * DH_A, HKV_A * DH_A, H_IDX * D_IDX, D_IDX, H_IDX, D_CQ, D_C, DR_B)
IN_OFFSETS = tuple(sum(IN_WIDTHS[:i]) for i in range(len(IN_WIDTHS)))
D_MIX = H_A * DH_A + H_B * DV_B

MM_DTYPE = jnp.bfloat16
F32 = jnp.float32
NEG_INF = float("-inf")
INT_MIN = -2 ** 31
LANES = 128
VMEM_LIMIT = 48 * 1024 * 1024
SAMPLE_VMEM_LIMIT = 56 * 1024 * 1024
NT_DIMS = (((1,), (1,)), ((), ()))

C_QA, C_KA, C_VA, C_QI, C_KIWI, C_CQ, C_CKV, C_KPE, W_IN = 0, 512, 640, 768, 1024, 1152, 1536, 1792, 1920
WI_LANE = 80
PE_LANE = DN_B
DQK_B = DN_B + DR_B
HEAD_PAD = LANES - DQK_B

IN_TILE = 256
ROUTE_TILE = 256
PEER_TOKEN_TILE = 512
PEER_EXPERT_TILE = 1024
Q_BLOCK = 128
CAUSAL_CLASSES = 4
B_CHUNK = 1024


def _seg_sum(sq, ind_ref):
    hi = sq.astype(MM_DTYPE)
    lo = (sq - hi.astype(F32)).astype(MM_DTYPE)
    ind = ind_ref[...]
    return jnp.dot(hi, ind, preferred_element_type=F32) + jnp.dot(lo, ind, preferred_element_type=F32)


def _rot(x, c, sm, sp, half):
    return x * c + pltpu.roll(x, LANES - half, 1) * sm + pltpu.roll(x, half, 1) * sp


def _in_proj_kernel(x_ref, rot_ref, gmix_ref, win_ref, ind_ref, indki_ref, g64_ref, gkiwi_ref, gcq_ref, wuq_ref,
                    gqb_ref, gckv_ref, wuk_ref, wuv_ref, gkb_ref,
                    qa_ref, ka_ref, va_ref, kiwi_ref, ckv_ref, kpe_ref,
                    kab_ref, vab_ref, kib_ref, qi_ref, qb_ref, kb_ref, vb_ref, qbk_ref):
    x = x_ref[...]
    h = x * lax.rsqrt(jnp.mean(x * x, axis=-1, keepdims=True) + EPS) * gmix_ref[...]
    proj = jnp.dot(h.astype(MM_DTYPE), win_ref[...], preferred_element_type=F32)
    c16, sm16, sp16 = rot_ref[0], rot_ref[1], rot_ref[2]
    c32, sm32, sp32 = rot_ref[3], rot_ref[4], rot_ref[5]
    half16, half32 = ROT_A // 2, DR_B // 2

    def norm_rot64(xc, g):
        rs = lax.rsqrt(_seg_sum(xc * xc, ind_ref) * (1.0 / DH_A) + EPS)
        return _rot(xc * rs * g, c16, sm16, sp16, half16)

    for j in range(H_A * DH_A // LANES):
        xc = proj[:, C_QA + j * LANES:C_QA + (j + 1) * LANES]
        qa_ref[:, j * LANES:(j + 1) * LANES] = norm_rot64(xc, g64_ref[0:1, :]).astype(qa_ref.dtype)
    ka = norm_rot64(proj[:, C_KA:C_KA + LANES], g64_ref[1:2, :])
    ka_ref[...] = ka
    kab_ref[...] = ka.astype(kab_ref.dtype)
    va = proj[:, C_VA:C_VA + LANES]
    va_ref[...] = va
    vab_ref[...] = va.astype(vab_ref.dtype)
    for j in range(H_IDX * D_IDX // LANES):
        xc = proj[:, C_QI + j * LANES:C_QI + (j + 1) * LANES]
        qi_ref[:, j * LANES:(j + 1) * LANES] = _rot(xc, c16, sm16, sp16, half16).astype(qi_ref.dtype)
    xc = proj[:, C_KIWI:C_KIWI + LANES]
    lane = lax.broadcasted_iota(jnp.int32, xc.shape, 1)
    rs = lax.rsqrt(_seg_sum(xc * xc, indki_ref) * (1.0 / D_IDX) + EPS)
    kiwi = _rot(xc * jnp.where(lane < D_IDX, rs, 1.0) * gkiwi_ref[...], c16, sm16, sp16, half16)
    kiwi_ref[...] = kiwi
    kib_ref[...] = kiwi[:, :D_IDX].astype(kib_ref.dtype)
    cq = proj[:, C_CQ:C_CQ + D_CQ]
    cqn = cq * lax.rsqrt(jnp.mean(cq * cq, axis=-1, keepdims=True) + EPS) * gcq_ref[...]
    qb = jnp.dot(cqn.astype(MM_DTYPE), wuq_ref[...], preferred_element_type=F32)
    ckv = proj[:, C_CKV:C_CKV + D_C]
    ckvn = ckv * lax.rsqrt(jnp.mean(ckv * ckv, axis=-1, keepdims=True) + EPS) * gckv_ref[...]
    ckv_ref[...] = ckvn
    kpe = _rot(proj[:, C_KPE:C_KPE + LANES], c32, sm32, sp32, half32)
    kpe_ref[...] = kpe
    ckvb = ckvn.astype(MM_DTYPE)
    knope = jnp.dot(ckvb, wuk_ref[...], preferred_element_type=F32)
    vb_ref[...] = jnp.dot(ckvb, wuv_ref[...], preferred_element_type=F32).astype(vb_ref.dtype)
    for hd in range(H_B):
        sl = slice(hd * LANES, (hd + 1) * LANES)
        qh = _rot(qb[:, sl], c32, sm32, sp32, half32)
        qh = qh * lax.rsqrt(jnp.sum(qh * qh, axis=-1, keepdims=True) * (1.0 / DQK_B) + EPS) * gqb_ref[...]
        qb_ref[:, sl] = qh.astype(qb_ref.dtype)
        qbk_ref[:, sl] = (qh * gkb_ref[...]).astype(qbk_ref.dtype)
        kh = knope[:, sl] + kpe
        kh = kh * lax.rsqrt(jnp.sum(kh * kh, axis=-1, keepdims=True) * (1.0 / DQK_B) + EPS) * gkb_ref[...]
        kb_ref[:, sl] = kh.astype(kb_ref.dtype)


def rotary_tables(pos):
    p = pos.shape[0]

    def ang(n_rot):
        inv = ROPE_THETA ** (-jnp.arange(n_rot // 2, dtype=F32) * 2.0 / n_rot)
        return pos[:, None] * inv[None, :]

    a16 = ang(ROT_A)
    cos16, sin16 = jnp.cos(a16), jnp.sin(a16)
    one48 = jnp.ones((p, DH_A - ROT_A), F32)
    zero48 = jnp.zeros((p, DH_A - ROT_A), F32)
    zero8 = jnp.zeros((p, ROT_A // 2), F32)
    c16 = jnp.tile(jnp.concatenate([cos16, cos16, one48], axis=1), (1, 2))
    sm16 = jnp.tile(jnp.concatenate([-sin16, zero8, zero48], axis=1), (1, 2))
    sp16 = jnp.tile(jnp.concatenate([zero8, sin16, zero48], axis=1), (1, 2))
    a32 = ang(DR_B)
    cos32, sin32 = jnp.cos(a32), jnp.sin(a32)
    one64, zero64 = jnp.ones((p, PE_LANE), F32), jnp.zeros((p, PE_LANE), F32)
    one32, zero32 = jnp.ones((p, HEAD_PAD), F32), jnp.zeros((p, HEAD_PAD), F32)
    zero16 = jnp.zeros((p, DR_B // 2), F32)
    c32 = jnp.concatenate([one64, cos32, cos32, one32], axis=1)
    sm32 = jnp.concatenate([zero64, -sin32, zero16, zero32], axis=1)
    sp32 = jnp.concatenate([zero64, zero16, sin32, zero32], axis=1)
    return jnp.stack([c16, sm16, sp16, c32, sm32, sp32], axis=0)


def in_proj_prepare(g_norm_mix, w_in, g_q_a, g_k_a, g_k_idx, g_cq, w_uq, g_ckv, w_uk, w_uv, g_q_b, g_k_b):
    def cols(i, left=0, right=0):
        return jnp.pad(w_in[:, IN_OFFSETS[i]:IN_OFFSETS[i] + IN_WIDTHS[i]], ((0, 0), (left, right)))

    win = jnp.concatenate([cols(0), cols(1), cols(2), cols(3), cols(4, 0, WI_LANE - D_IDX),
                           cols(5, 0, LANES - WI_LANE - H_IDX), cols(6), cols(7),
                           cols(8, PE_LANE, HEAD_PAD)], axis=1).astype(MM_DTYPE)
    lane = jnp.arange(LANES)
    ind = (lane[:, None] // DH_A == lane[None, :] // DH_A).astype(MM_DTYPE)
    indki = ((lane[:, None] < D_IDX) & (lane[None, :] < D_IDX)).astype(MM_DTYPE)
    g64 = jnp.stack([jnp.tile(g_q_a, 2), jnp.tile(g_k_a, 2)], axis=0)
    gkiwi = jnp.concatenate([g_k_idx, jnp.zeros((WI_LANE - D_IDX,), F32),
                             jnp.full((H_IDX,), (H_IDX * D_IDX) ** -0.5, F32),
                             jnp.zeros((LANES - WI_LANE - H_IDX,), F32)]).reshape(1, LANES)
    wuq = jnp.pad(w_uq.reshape(D_CQ, H_B, DQK_B), ((0, 0), (0, 0), (0, HEAD_PAD))).reshape(D_CQ, H_B * LANES)
    wuk = jnp.pad(w_uk.reshape(D_C, H_B, DN_B), ((0, 0), (0, 0), (0, LANES - DN_B))).reshape(D_C, H_B * LANES)
    gqb = jnp.pad(g_q_b, (0, HEAD_PAD)).reshape(1, LANES)
    gkb = jnp.pad(g_k_b, (0, HEAD_PAD)).reshape(1, LANES)
    return (g_norm_mix.reshape(1, D_MODEL), win, ind, indki, g64, gkiwi, g_cq.reshape(1, D_CQ),
            wuq.astype(MM_DTYPE), gqb, g_ckv.reshape(1, D_C), wuk.astype(MM_DTYPE), w_uv.astype(MM_DTYPE), gkb)


def in_proj(x2d, rot, prepared, *, tile):
    t = x2d.shape[0]
    n_rot_blocks = rot.shape[1] // tile
    row = lambda i: (i, 0)
    full = lambda a: pl.BlockSpec(a.shape, lambda i: (0,) * a.ndim)
    widths = [(H_A * DH_A, MM_DTYPE), (LANES, F32), (LANES, F32), (LANES, F32), (D_C, F32), (LANES, F32),
              (LANES, MM_DTYPE), (LANES, MM_DTYPE), (D_IDX, MM_DTYPE), (H_IDX * D_IDX, MM_DTYPE),
              (H_B * LANES, MM_DTYPE), (H_B * LANES, MM_DTYPE), (H_B * DV_B, MM_DTYPE), (H_B * LANES, MM_DTYPE)]
    return pl.pallas_call(
        _in_proj_kernel,
        grid=(t // tile,),
        in_specs=[pl.BlockSpec((tile, D_MODEL), row),
                  pl.BlockSpec((6, tile, LANES), lambda i: (0, i % n_rot_blocks, 0))] + [full(a) for a in prepared],
        out_specs=[pl.BlockSpec((tile, w), row) for w, _ in widths],
        out_shape=[jax.ShapeDtypeStruct((t, w), d) for w, d in widths],
        compiler_params=pltpu.CompilerParams(dimension_semantics=("parallel",), vmem_limit_bytes=VMEM_LIMIT),
        name="in_proj",
    )(x2d, rot, *prepared)


def _sortable_key(x):
    b = pltpu.bitcast(x, jnp.int32)
    return jnp.where(b < 0, b ^ 0x7FFFFFFF, b)


def _count(mask):
    return jnp.sum(jnp.where(mask, 1.0, 0.0), axis=-1, keepdims=True)


def _kth_largest_key(key, k):
    kf = float(k)
    t0 = jnp.where(_count(key >= 0) >= kf, 0, INT_MIN).astype(jnp.int32)

    def body(i, t):
        cand = t + jnp.left_shift(jnp.int32(1), 30 - i)
        return jnp.where(_count(key >= cand) >= kf, cand, t)

    return lax.fori_loop(0, 31, body, t0)


def _topk_mask(score, k, tri_ref):
    key = _sortable_key(score)
    t = _kth_largest_key(key, k)
    gt = key > t
    eq = key == t
    need = float(k) - _count(gt)
    carry = jnp.zeros_like(need)
    parts = []
    for c in range(score.shape[1] // LANES):
        sl = slice(c * LANES, (c + 1) * LANES)
        eq_c = jnp.where(eq[:, sl], 1.0, 0.0)
        incl = jnp.dot(eq_c.astype(MM_DTYPE), tri_ref[...], preferred_element_type=F32)
        rank = incl - eq_c + carry
        parts.append(gt[:, sl] | (eq[:, sl] & (rank < need)))
        carry = carry + jnp.sum(eq_c, axis=-1, keepdims=True)
    return jnp.concatenate(parts, axis=1)


def _tie_rank_matrix():
    lane = jnp.arange(LANES)
    return (lane[:, None] <= lane[None, :]).astype(MM_DTYPE)


def _masked_attend(q, k, v, mask, scale):
    s = lax.dot_general(q, k, NT_DIMS, preferred_element_type=F32) * scale
    s = jnp.where(mask, s, NEG_INF)
    p = jnp.exp(s - jnp.max(s, axis=-1, keepdims=True))
    l = jnp.sum(p, axis=-1, keepdims=True)
    return jnp.dot(p.astype(MM_DTYPE), v, preferred_element_type=F32) / l


def _prompt_attn_kernel(qa_ref, qi_ref, kiwi_ref, qb_ref, ki_ref, ka_ref, va_ref, kb_ref, vb_ref, tri_ref, o_ref,
                        *, k_top, first_block):
    qn = qa_ref.shape[1]
    s_len = ki_ref.shape[1]
    q0 = (first_block + pl.program_id(1)) * qn
    qpos = q0 + lax.broadcasted_iota(jnp.int32, (qn, s_len), 0)
    kpos = lax.broadcasted_iota(jnp.int32, (qn, s_len), 1)
    causal = kpos <= qpos

    ki = ki_ref[0]
    qi = qi_ref[0]
    kiwi = kiwi_ref[0]
    score = jnp.zeros((qn, s_len), F32)
    for h in range(H_IDX):
        s = lax.dot_general(qi[:, h * D_IDX:(h + 1) * D_IDX], ki, NT_DIMS, preferred_element_type=F32)
        score = score + kiwi[:, WI_LANE + h:WI_LANE + h + 1] * jnp.maximum(s, 0.0)
    score = jnp.where(causal, score, NEG_INF)
    sel = _topk_mask(score, k_top, tri_ref) & causal

    qa = qa_ref[0]
    ka = ka_ref[0]
    va = va_ref[0]
    outs = []
    group = H_A // HKV_A
    for h in range(H_A):
        g = h // group
        outs.append(_masked_attend(qa[:, h * DH_A:(h + 1) * DH_A], ka[:, g * DH_A:(g + 1) * DH_A],
                                   va[:, g * DH_A:(g + 1) * DH_A], sel, DH_A ** -0.5))
    qb = qb_ref[0]
    for h in range(H_B):
        outs.append(_masked_attend(qb[:, h * LANES:(h + 1) * LANES], kb_ref[0, :, h * LANES:(h + 1) * LANES],
                                   vb_ref[0, :, h * DV_B:(h + 1) * DV_B], causal, DQK_B ** -0.5))
    for j in range(len(outs) // 2):
        o_ref[0, :, j * LANES:(j + 1) * LANES] = jnp.concatenate(outs[2 * j:2 * j + 2], axis=1).astype(o_ref.dtype)


def prompt_attention(qa, qi, kiwi, qb, ki, ka, va, kb, vb, *, q_block, n_classes):
    n, s_len, _ = qa.shape
    k_top = min(TOPK_MAX, s_len // 4)
    tri = _tie_rank_matrix()
    class_len = s_len // n_classes
    blocks = class_len // q_block
    outs = []
    for c in range(n_classes):
        first = c * blocks
        s_vis = (c + 1) * class_len
        qspec = lambda a: pl.BlockSpec((1, q_block, a.shape[2]), lambda b, i: (b, first + i, 0))
        kspec = lambda a: pl.BlockSpec((1, s_vis, a.shape[2]), lambda b, i: (b, 0, 0))
        outs.append(pl.pallas_call(
            functools.partial(_prompt_attn_kernel, k_top=k_top, first_block=first),
            grid=(n, blocks),
            in_specs=[qspec(qa), qspec(qi), qspec(kiwi), qspec(qb), kspec(ki), kspec(ka), kspec(va), kspec(kb),
                      kspec(vb), pl.BlockSpec(tri.shape, lambda b, i: (0, 0))],
            out_specs=pl.BlockSpec((1, q_block, D_MIX), lambda b, i: (b, i, 0)),
            out_shape=jax.ShapeDtypeStruct((n, class_len, D_MIX), MM_DTYPE),
            compiler_params=pltpu.CompilerParams(dimension_semantics=("parallel", "parallel"),
                                                 vmem_limit_bytes=VMEM_LIMIT),
            name="prompt_attention",
        )(qa, qi, kiwi, qb, ki, ka, va, kb, vb, tri))
    return jnp.concatenate(outs, axis=1)


def _page_copy(pt_ref, n, p, slot, cache, buf, sem, keys_last):
    phys = pt_ref[n * N_PAGES + p]
    keys = pl.ds(pl.multiple_of(p * PAGE_SIZE, PAGE_SIZE), PAGE_SIZE)
    lead = (slice(None),) * (len(buf.shape) - 2)
    dst = buf.at[(slot,) + lead + (keys,)] if keys_last else buf.at[slot, keys]
    return pltpu.make_async_copy(cache.at[phys], dst, sem.at[slot])


def _start_pages(pt_ref, n, slot, streams):
    def body(p, carry):
        for cache, buf, sem, keys_last in streams:
            _page_copy(pt_ref, n, p, slot, cache, buf, sem, keys_last).start()
        return carry
    lax.fori_loop(0, N_PAGES, body, 0)


def _wait_pages(pt_ref, n, slot, streams):
    def body(p, carry):
        for cache, buf, sem, keys_last in streams:
            _page_copy(pt_ref, n, p, slot, cache, buf, sem, keys_last).wait()
        return carry
    lax.fori_loop(0, N_PAGES, body, 0)


def _fetch_sequence_pages(pt_ref, streams):
    n = pl.program_id(0)
    slot = n % 2

    @pl.when(n == 0)
    def _():
        _start_pages(pt_ref, n, slot, streams)

    @pl.when(n + 1 < pl.num_programs(0))
    def _():
        _start_pages(pt_ref, n + 1, 1 - slot, streams)

    _wait_pages(pt_ref, n, slot, streams)
    return slot


def _pad_rows(a, rows):
    return jnp.concatenate([a, jnp.zeros((rows - a.shape[0], a.shape[1]), a.dtype)], axis=0)


def _new_key_mask(t_rows, reps):
    tok = lax.broadcasted_iota(jnp.int32, (reps * t_rows, LANES), 0) % t_rows
    j = lax.broadcasted_iota(jnp.int32, (reps * t_rows, LANES), 1)
    return j <= tok


def _sample_a_kernel(pt_ref, qa_ref, qi_ref, kiwi_ref, kin_ref, kan_ref, van_ref, tri_ref,
                     ckidx_hbm, ck_hbm, cv_hbm, o_ref, kidx_buf, k_buf, v_buf, sem_i, sem_k, sem_v):
    slot = _fetch_sequence_pages(pt_ref, ((ckidx_hbm, kidx_buf, sem_i, True), (ck_hbm, k_buf, sem_k, True),
                                          (cv_hbm, v_buf, sem_v, True)))
    t_new = qa_ref.shape[1]
    qi = qi_ref[0]
    kiwi = kiwi_ref[0]
    qi_rows = jnp.concatenate([qi[:, h * D_IDX:(h + 1) * D_IDX] for h in range(H_IDX)], axis=0)
    sp_all = jnp.dot(qi_rows, kidx_buf[slot].astype(MM_DTYPE), preferred_element_type=F32)
    sn_all = lax.dot_general(qi_rows, _pad_rows(kin_ref[0], LANES), NT_DIMS, preferred_element_type=F32)
    score_p = jnp.zeros((t_new, PAST_LEN), F32)
    score_n = jnp.zeros((t_new, LANES), F32)
    for h in range(H_IDX):
        wh = kiwi[:, WI_LANE + h:WI_LANE + h + 1]
        score_p = score_p + wh * jnp.maximum(sp_all[h * t_new:(h + 1) * t_new], 0.0)
        score_n = score_n + wh * jnp.maximum(sn_all[h * t_new:(h + 1) * t_new], 0.0)
    vis_n = _new_key_mask(t_new, 1)
    score = jnp.concatenate([score_p, jnp.where(vis_n, score_n, NEG_INF)], axis=1)
    k_top = min(TOPK_MAX, (PAST_LEN + t_new) // 4)
    visible = jnp.concatenate([jnp.full((t_new, PAST_LEN), True), vis_n], axis=1)
    sel = _topk_mask(score, k_top, tri_ref) & visible

    group = H_A // HKV_A
    sel_g = jnp.concatenate([sel] * group, axis=0)
    qa = qa_ref[0]
    outs = []
    for g in range(HKV_A):
        gs = slice(g * DH_A, (g + 1) * DH_A)
        q_g = jnp.concatenate([qa[:, (g * group + j) * DH_A:(g * group + j + 1) * DH_A] for j in range(group)], axis=0)
        kn_g = _pad_rows(kan_ref[0][:, gs], LANES)
        vn_g = _pad_rows(van_ref[0][:, gs], LANES)
        s = jnp.concatenate([jnp.dot(q_g, k_buf[slot, g].astype(MM_DTYPE), preferred_element_type=F32),
                             lax.dot_general(q_g, kn_g, NT_DIMS, preferred_element_type=F32)], axis=1) * DH_A ** -0.5
        s = jnp.where(sel_g, s, NEG_INF)
        p = jnp.exp(s - jnp.max(s, axis=-1, keepdims=True))
        l = jnp.sum(p, axis=-1, keepdims=True)
        pb = p.astype(MM_DTYPE)
        o_g = (lax.dot_general(pb[:, :PAST_LEN], v_buf[slot, g].astype(MM_DTYPE), NT_DIMS, preferred_element_type=F32)
               + jnp.dot(pb[:, PAST_LEN:], vn_g, preferred_element_type=F32)) / l
        outs += [o_g[j * t_new:(j + 1) * t_new] for j in range(group)]
    for j in range(len(outs) // 2):
        o_ref[0, :, j * LANES:(j + 1) * LANES] = jnp.concatenate(outs[2 * j:2 * j + 2], axis=1).astype(o_ref.dtype)


def sample_attention_a(page_table, qa, qi, kiwi, ki_new, ka_new, va_new, cache_kidx, cache_k, cache_v):
    n, t_new, _ = qa.shape
    tri = _tie_rank_matrix()
    seq = lambda a: pl.BlockSpec((1, t_new, a.shape[2]), lambda i, pt: (i, 0, 0))
    hbm = pl.BlockSpec(memory_space=pl.ANY)
    return pl.pallas_call(
        _sample_a_kernel,
        grid_spec=pltpu.PrefetchScalarGridSpec(
            num_scalar_prefetch=1,
            grid=(n,),
            in_specs=[seq(qa), seq(qi), seq(kiwi), seq(ki_new), seq(ka_new), seq(va_new),
                      pl.BlockSpec(tri.shape, lambda i, pt: (0, 0)), hbm, hbm, hbm],
            out_specs=pl.BlockSpec((1, t_new, H_A * DH_A), lambda i, pt: (i, 0, 0)),
            scratch_shapes=[pltpu.VMEM((2, D_IDX, PAST_LEN), F32),
                            pltpu.VMEM((2, HKV_A, DH_A, PAST_LEN), F32),
                            pltpu.VMEM((2, HKV_A, DH_A, PAST_LEN), F32),
                            pltpu.SemaphoreType.DMA((2,)), pltpu.SemaphoreType.DMA((2,)),
                            pltpu.SemaphoreType.DMA((2,))]),
        out_shape=jax.ShapeDtypeStruct((n, t_new, H_A * DH_A), MM_DTYPE),
        compiler_params=pltpu.CompilerParams(dimension_semantics=("arbitrary",),
                                             vmem_limit_bytes=SAMPLE_VMEM_LIMIT),
        name="sample_attention_a",
    )(page_table.reshape(-1), qa, qi, kiwi, ki_new, ka_new, va_new, tri,
      jnp.transpose(cache_kidx, (0, 2, 1)), jnp.transpose(cache_k, (0, 2, 3, 1)), jnp.transpose(cache_v, (0, 2, 3, 1)))


def _sample_b_kernel(pt_ref, qb_ref, qbk_ref, kbn_ref, ckvn_ref, wukT_ref, wuv_ref,
                     ckv_hbm, kpe_hbm, o_ref, ckv_buf, kpe_buf, sem_c, sem_p):
    slot = _fetch_sequence_pages(pt_ref, ((ckv_hbm, ckv_buf, sem_c, False), (kpe_hbm, kpe_buf, sem_p, True)))
    t_new = qb_ref.shape[1]
    rows = H_B * t_new
    qb = qb_ref[0]
    qbk = qbk_ref[0]
    scale = DQK_B ** -0.5

    def update(carry, s, latent):
        m, l, acc = carry
        m_new = jnp.maximum(m, jnp.max(s, axis=-1, keepdims=True))
        alpha = jnp.exp(m - m_new)
        p = jnp.exp(s - m_new)
        l = alpha * l + jnp.sum(p, axis=-1, keepdims=True)
        acc = alpha * acc + jnp.dot(p.astype(MM_DTYPE), latent, preferred_element_type=F32)
        return m_new, l, acc

    def chunk(c, carry):
        ks = pl.ds(pl.multiple_of(c * B_CHUNK, B_CHUNK), B_CHUNK)
        ckv_c = ckv_buf[slot, ks, :].astype(MM_DTYPE)
        kpeT = kpe_buf[slot, :, ks]
        knopeT = lax.dot_general(wukT_ref[...], ckv_c, NT_DIMS, preferred_element_type=F32)
        pe_ss = jnp.sum(kpeT * kpeT, axis=0, keepdims=True)
        kpeT_b = kpeT.astype(MM_DTYPE)
        parts = []
        for h in range(H_B):
            kn = knopeT[h * DN_B:(h + 1) * DN_B]
            rs = lax.rsqrt((jnp.sum(kn * kn, axis=0, keepdims=True) + pe_ss) * (1.0 / DQK_B) + EPS)
            kT = jnp.concatenate([kn.astype(MM_DTYPE), kpeT_b], axis=0)
            parts.append(jnp.dot(qbk[:, h * LANES:h * LANES + DQK_B], kT, preferred_element_type=F32) * rs)
        return update(carry, jnp.concatenate(parts, axis=0) * scale, ckv_c)

    init = (jnp.full((rows, 1), NEG_INF, F32), jnp.zeros((rows, 1), F32), jnp.zeros((rows, D_C), F32))
    carry = lax.fori_loop(0, PAST_LEN // B_CHUNK, chunk, init)
    kbn = _pad_rows(kbn_ref[0], LANES)
    s_new = jnp.concatenate([lax.dot_general(qb[:, h * LANES:(h + 1) * LANES], kbn[:, h * LANES:(h + 1) * LANES],
                                             NT_DIMS, preferred_element_type=F32) for h in range(H_B)], axis=0) * scale
    s_new = jnp.where(_new_key_mask(t_new, H_B), s_new, NEG_INF)
    m, l, acc = update(carry, s_new, _pad_rows(ckvn_ref[0].astype(MM_DTYPE), LANES))
    lat = (acc / l).astype(MM_DTYPE)
    outs = [jnp.dot(lat[h * t_new:(h + 1) * t_new], wuv_ref[:, h * DV_B:(h + 1) * DV_B],
                    preferred_element_type=F32) for h in range(H_B)]
    for j in range(H_B // 2):
        o_ref[0, :, j * LANES:(j + 1) * LANES] = jnp.concatenate(outs[2 * j:2 * j + 2], axis=1).astype(o_ref.dtype)


def sample_attention_b(page_table, qb, qbk, kb_new, ckv_new, wukT, wuv, cache_ckv, cache_kpe):
    n, t_new, _ = qb.shape
    seq = lambda a: pl.BlockSpec((1, t_new, a.shape[2]), lambda i, pt: (i, 0, 0))
    full = lambda a: pl.BlockSpec(a.shape, lambda i, pt: (0,) * a.ndim)
    hbm = pl.BlockSpec(memory_space=pl.ANY)
    return pl.pallas_call(
        _sample_b_kernel,
        grid_spec=pltpu.PrefetchScalarGridSpec(
            num_scalar_prefetch=1,
            grid=(n,),
            in_specs=[seq(qb), seq(qbk), seq(kb_new), seq(ckv_new), full(wukT), full(wuv), hbm, hbm],
            out_specs=pl.BlockSpec((1, t_new, H_B * DV_B), lambda i, pt: (i, 0, 0)),
            scratch_shapes=[pltpu.VMEM((2, PAST_LEN, D_C), F32),
                            pltpu.VMEM((2, DR_B, PAST_LEN), F32),
                            pltpu.SemaphoreType.DMA((2,)), pltpu.SemaphoreType.DMA((2,))]),
        out_shape=jax.ShapeDtypeStruct((n, t_new, H_B * DV_B), MM_DTYPE),
        compiler_params=pltpu.CompilerParams(dimension_semantics=("arbitrary",),
                                             vmem_limit_bytes=SAMPLE_VMEM_LIMIT),
        name="sample_attention_b",
    )(page_table.reshape(-1), qb, qbk, kb_new, ckv_new, wukT, wuv, cache_ckv, jnp.transpose(cache_kpe, (0, 2, 1)))


LOG2E = 1.4426950408889634
GELU_A = 0.7978845608028654
GELU_B = GELU_A * 0.044715


def _peer_route_kernel(x_ref, o_ref, wout_ref, g_ref, wqT_ref, subk_ref,
                       x1_ref, h2T_ref, sT_ref, thr_ref, top_ref):
    x1 = x_ref[...] + jnp.dot(o_ref[...].astype(MM_DTYPE), wout_ref[...], preferred_element_type=F32)
    x1_ref[...] = x1
    h2 = x1 * lax.rsqrt(jnp.mean(x1 * x1, axis=-1, keepdims=True) + EPS) * g_ref[...]
    h2b = h2.astype(MM_DTYPE)
    h2T_ref[...] = h2.T.astype(h2T_ref.dtype)
    qT = lax.dot_general(wqT_ref[...], h2b, NT_DIMS, preferred_element_type=F32)
    half = D_KEY // 2
    for h in range(PEER_HEADS):
        s2 = []
        for p in range(2):
            r = (h * 2 + p) * half
            sT = jnp.dot(subk_ref[h * 2 + p], qT[r:r + half, :].astype(MM_DTYPE), preferred_element_type=F32) * LOG2E
            s2.append(sT)
            work = sT
            for k in range(PEER_TOPK + 1):
                m = jnp.max(work, axis=0, keepdims=True)
                top_ref[p, k:k + 1, :] = m
                work = jnp.where(work == m, NEG_INF, work)
        v0 = top_ref[0]
        v1 = top_ref[1]
        parts = [v0[0:1, :] + v1[0:PEER_TOPK, :]]
        for a in range(1, 8):
            parts.append(v0[a:a + 1, :] + v1[0:8, :])
        parts.append(v0[8:16, :] + v1[0:1, :])
        cand = jnp.concatenate(parts, axis=0)
        work = cand
        for k in range(PEER_TOPK + 1):
            m = jnp.max(work, axis=0, keepdims=True)
            if k == PEER_TOPK - 1:
                thr = m
            work = jnp.where(work == m, NEG_INF, work)
        nxt = jnp.maximum(m, jnp.maximum(v0[PEER_TOPK:PEER_TOPK + 1, :] + v1[0:1, :],
                                         v0[0:1, :] + v1[PEER_TOPK:PEER_TOPK + 1, :]))
        top = v0[0:1, :] + v1[0:1, :]
        z = jnp.sum(jnp.where(cand >= thr, jnp.exp2(cand - top), 0.0), axis=0, keepdims=True)
        c = top + jnp.log2(z) + 1.0
        sT_ref[h] = s2[0] - c
        sT_ref[PEER_HEADS + h] = s2[1]
        thr_ref[h:h + 1, :] = 0.5 * (thr + nxt) - c


def peer_route(x, o, wout_b, g_ffn, wqT_b, subk_b, *, tile):
    t = x.shape[0]
    row = lambda i: (i, 0)
    col = lambda i: (0, i)
    fixed2 = lambda i: (0, 0)
    return pl.pallas_call(
        _peer_route_kernel,
        grid=(t // tile,),
        in_specs=[pl.BlockSpec((tile, D_MODEL), row),
                  pl.BlockSpec((tile, o.shape[1]), row),
                  pl.BlockSpec(wout_b.shape, fixed2),
                  pl.BlockSpec((1, D_MODEL), fixed2),
                  pl.BlockSpec(wqT_b.shape, fixed2),
                  pl.BlockSpec(subk_b.shape, lambda i: (0, 0, 0))],
        out_specs=[pl.BlockSpec((tile, D_MODEL), row),
                   pl.BlockSpec((D_MODEL, tile), col),
                   pl.BlockSpec((2 * PEER_HEADS, N_KEYS, tile), lambda i: (0, 0, i)),
                   pl.BlockSpec((PEER_HEADS, tile), col)],
        out_shape=[jax.ShapeDtypeStruct((t, D_MODEL), F32),
                   jax.ShapeDtypeStruct((D_MODEL, t), MM_DTYPE),
                   jax.ShapeDtypeStruct((2 * PEER_HEADS, N_KEYS, t), F32),
                   jax.ShapeDtypeStruct((PEER_HEADS, t), F32)],
        scratch_shapes=[pltpu.VMEM((2, PEER_TOPK + 8, tile), F32)],
        compiler_params=pltpu.CompilerParams(dimension_semantics=("parallel",), vmem_limit_bytes=VMEM_LIMIT),
        name="peer_route",
    )(x, o, wout_b, g_ffn.reshape(1, D_MODEL), wqT_b, subk_b)


def _peer_dense_kernel(h2T_ref, u_ref, vT_ref, s0_ref, s1_ref, thr_ref, x1_ref, y_ref, acc_ref):
    ei = pl.program_id(1)
    rows_per_step = s0_ref.shape[1]

    @pl.when(ei == 0)
    def _():
        acc_ref[...] = jnp.zeros_like(acc_ref)

    hT = jnp.dot(u_ref[...], h2T_ref[...], preferred_element_type=F32)
    ws = []
    for ii in range(rows_per_step):
        hid = hT[ii * N_KEYS:(ii + 1) * N_KEYS, :]
        gate = jnp.zeros_like(hid)
        for h in range(PEER_HEADS):
            arg = s0_ref[h, ii:ii + 1, :] + s1_ref[h]
            gate = gate + jnp.where(arg >= thr_ref[h:h + 1, :], jnp.exp2(arg), 0.0)
        gx = gate * hid
        ws.append((gx + gx * jnp.tanh(hid * (GELU_A + GELU_B * (hid * hid)))).astype(MM_DTYPE))
    w = jnp.concatenate(ws, axis=0)
    acc_ref[...] += jnp.dot(vT_ref[...], w, preferred_element_type=F32)

    @pl.when(ei == pl.num_programs(1) - 1)
    def _():
        y_ref[...] = x1_ref[...] + acc_ref[...].T


def peer_dense(h2T, u_b, vT_b, sT, thr, x1, *, tile, etile):
    t = x1.shape[0]
    n_exp = u_b.shape[0]
    rows = etile // N_KEYS
    return pl.pallas_call(
        _peer_dense_kernel,
        grid=(t // tile, n_exp // etile),
        in_specs=[pl.BlockSpec((D_MODEL, tile), lambda ti, ei: (0, ti)),
                  pl.BlockSpec((etile, D_MODEL), lambda ti, ei: (ei, 0)),
                  pl.BlockSpec((D_MODEL, etile), lambda ti, ei: (0, ei)),
                  pl.BlockSpec((PEER_HEADS, rows, tile), lambda ti, ei: (0, ei, ti)),
                  pl.BlockSpec((PEER_HEADS, N_KEYS, tile), lambda ti, ei: (1, 0, ti)),
                  pl.BlockSpec((PEER_HEADS, tile), lambda ti, ei: (0, ti)),
                  pl.BlockSpec((tile, D_MODEL), lambda ti, ei: (ti, 0))],
        out_specs=pl.BlockSpec((tile, D_MODEL), lambda ti, ei: (ti, 0)),
        out_shape=jax.ShapeDtypeStruct((t, D_MODEL), F32),
        scratch_shapes=[pltpu.VMEM((D_MODEL, tile), F32)],
        compiler_params=pltpu.CompilerParams(dimension_semantics=("parallel", "arbitrary"),
                                             vmem_limit_bytes=VMEM_LIMIT),
        name="peer_dense",
    )(h2T, u_b, vT_b, sT, sT, thr, x1)


def peer_prepare(w_out, w_peer_q, peer_sub_keys, peer_u, peer_v):
    return (w_out.astype(MM_DTYPE), w_peer_q.T.astype(MM_DTYPE),
            peer_sub_keys.reshape(2 * PEER_HEADS, N_KEYS, D_KEY // 2).astype(MM_DTYPE),
            peer_u.astype(MM_DTYPE), peer_v.T.astype(MM_DTYPE))


def peer_layer(x2d, o2d, g_ffn, prepared):
    wout_b, wqT_b, subk_b, u_b, vT_b = prepared
    x1, h2T, sT, thr = peer_route(x2d, o2d, wout_b, g_ffn, wqT_b, subk_b, tile=ROUTE_TILE)
    return peer_dense(h2T, u_b, vT_b, sT, thr, x1, tile=PEER_TOKEN_TILE, etile=PEER_EXPERT_TILE)


def kernel(x_prompt, x_sample, cache_a_k, cache_a_v, cache_a_kidx, cache_b_ckv, cache_b_kpe, page_table,
           g_norm_mix, w_in, g_q_a, g_k_a, g_k_idx, g_cq, w_uq, g_ckv, w_uk, w_uv, g_q_b, g_k_b,
           w_out, g_norm_ffn, w_peer_q, peer_sub_keys, peer_u, peer_v):
    n_p, s_len, _ = x_prompt.shape
    n_s, t_new, _ = x_sample.shape
    in_prep = in_proj_prepare(g_norm_mix, w_in, g_q_a, g_k_a, g_k_idx, g_cq, w_uq, g_ckv, w_uk, w_uv, g_q_b, g_k_b)
    wuv_b = in_prep[11]
    peer_prep = peer_prepare(w_out, w_peer_q, peer_sub_keys, peer_u, peer_v)

    rot_p = rotary_tables(jnp.arange(s_len, dtype=F32))
    (_qa, ka, va, kiwi, ckv, kpe, ka_b, va_b, ki_b, qi, qb, kb, vb, _) = in_proj(
        x_prompt.reshape(-1, D_MODEL), rot_p, in_prep, tile=IN_TILE)
    seq_p = lambda a: a.reshape(n_p, s_len, a.shape[-1])
    o_p = prompt_attention(seq_p(_qa), seq_p(qi), seq_p(kiwi), seq_p(qb), seq_p(ki_b), seq_p(ka_b), seq_p(va_b),
                           seq_p(kb), seq_p(vb), q_block=Q_BLOCK, n_classes=CAUSAL_CLASSES)
    yp = peer_layer(x_prompt.reshape(-1, D_MODEL), o_p.reshape(-1, D_MIX), g_norm_ffn, peer_prep)
    prompt_rows = (ka.reshape(n_p, s_len, HKV_A, DH_A), va.reshape(n_p, s_len, HKV_A, DH_A),
                   kiwi[:, :D_IDX].reshape(n_p, s_len, D_IDX), ckv.reshape(n_p, s_len, D_C),
                   kpe[:, PE_LANE:PE_LANE + DR_B].reshape(n_p, s_len, DR_B))

    rot_s = rotary_tables(jnp.tile(PAST_LEN + jnp.arange(t_new, dtype=F32), IN_TILE // t_new))
    (_qa, ka, va, kiwi, ckv, kpe, ka_b, va_b, ki_b, qi, qb, kb, vb, qbk) = in_proj(
        x_sample.reshape(-1, D_MODEL), rot_s, in_prep, tile=IN_TILE)
    seq_s = lambda a: a.reshape(n_s, t_new, a.shape[-1])
    o_a = sample_attention_a(page_table, seq_s(_qa), seq_s(qi), seq_s(kiwi), seq_s(ki_b), seq_s(ka_b), seq_s(va_b),
                             cache_a_kidx, cache_a_k, cache_a_v)
    o_b = sample_attention_b(page_table, seq_s(qb), seq_s(qbk), seq_s(kb), seq_s(ckv), w_uk.T.astype(MM_DTYPE), wuv_b,
                             cache_b_ckv, cache_b_kpe)
    o_s = jnp.concatenate([o_a, o_b], axis=-1)
    ys = peer_layer(x_sample.reshape(-1, D_MODEL), o_s.reshape(-1, D_MIX), g_norm_ffn, peer_prep)
    sample_rows = (ka.reshape(n_s, t_new, HKV_A, DH_A), va.reshape(n_s, t_new, HKV_A, DH_A),
                   kiwi[:, :D_IDX].reshape(n_s, t_new, D_IDX), ckv.reshape(n_s, t_new, D_C),
                   kpe[:, PE_LANE:PE_LANE + DR_B].reshape(n_s, t_new, DR_B))
    return (yp.reshape(x_prompt.shape), ys.reshape(x_sample.shape)) + prompt_rows + sample_rows
```

```python
import functools

import jax
import jax.numpy as jnp
from jax import lax
from jax.experimental import pallas as pl
from jax.experimental.pallas import tpu as pltpu

D_MODEL = 1024
PAST_LEN = 8192
PAGE_SIZE = 128
N_PAGES = PAST_LEN // PAGE_SIZE
ROPE_THETA = 500000.0
EPS = 1e-6
H_A = 8
HKV_A = 2
DH_A = 64
ROT_A = DH_A // 4
H_IDX = 4
D_IDX = 64
TOPK_MAX = 256
H_B = 8
DN_B = 64
DR_B = 32
DV_B = 64
D_CQ = 384
D_C = 256
PEER_HEADS = 8
N_KEYS = 128
D_KEY = 128
PEER_TOPK = 16
IN_WIDTHS = (H_A * DH_A, HKV_A * DH_A, HKV_A * DH_A, H_IDX * D_IDX, D_IDX, H_IDX, D_CQ, D_C, DR_B)
IN_OFFSETS = tuple(sum(IN_WIDTHS[:i]) for i in range(len(IN_WIDTHS)))
D_MIX = H_A * DH_A + H_B * DV_B

MM_DTYPE = jnp.bfloat16
F32 = jnp.float32
NEG_INF = float("-inf")
INT_MIN = -2 ** 31
LANES = 128
SUBLANES = 8
VMEM_LIMIT = 48 * 1024 * 1024
SAMPLE_VMEM_LIMIT = 56 * 1024 * 1024
NT_DIMS = (((1,), (1,)), ((), ()))

C_QA, C_KA, C_VA, C_QI, C_KIWI, C_CQ, C_CKV, C_KPE, W_IN = 0, 512, 640, 768, 1024, 1152, 1536, 1792, 1920
WI_LANE = 80
PE_LANE = DN_B
DQK_B = DN_B + DR_B
HEAD_PAD = LANES - DQK_B

IN_TILE = 256
ROUTE_TILE = 256
PEER_TOKEN_TILE = 512
PEER_EXPERT_TILE = 1024
Q_BLOCK = 128
CAUSAL_CLASSES = 8
B_CHUNK = 1024
PAGE_LOOP_UNROLL = 8
PEER_CANDIDATES = 80


def _seg_sum(sq, ind_ref):
    hi = sq.astype(MM_DTYPE)
    lo = (sq - hi.astype(F32)).astype(MM_DTYPE)
    ind = ind_ref[...]
    return jnp.dot(hi, ind, preferred_element_type=F32) + jnp.dot(lo, ind, preferred_element_type=F32)


def _rot(x, c, sm, sp, half):
    return x * c + pltpu.roll(x, LANES - half, 1) * sm + pltpu.roll(x, half, 1) * sp


def _in_proj_kernel(x_ref, rot_ref, gmix_ref, win_ref, ind_ref, indki_ref, g64_ref, gkiwi_ref, gcq_ref, wuq_ref,
                    gqb_ref, gckv_ref, wuk_ref, wuv_ref, gkb_ref,
                    qa_ref, ka_ref, va_ref, kiwi_ref, ckv_ref, kpe_ref,
                    kab_ref, vab_ref, kib_ref, qi_ref, qb_ref, kb_ref, vb_ref, qbk_ref):
    x = x_ref[...]
    h = x * lax.rsqrt(jnp.mean(x * x, axis=-1, keepdims=True) + EPS) * gmix_ref[...]
    proj = jnp.dot(h.astype(MM_DTYPE), win_ref[...], preferred_element_type=F32)
    c16, sm16, sp16 = rot_ref[0], rot_ref[1], rot_ref[2]
    c32, sm32, sp32 = rot_ref[3], rot_ref[4], rot_ref[5]
    half16, half32 = ROT_A // 2, DR_B // 2

    def norm_rot64(xc, g):
        rs = lax.rsqrt(_seg_sum(xc * xc, ind_ref) * (1.0 / DH_A) + EPS)
        return _rot(xc * rs * g, c16, sm16, sp16, half16)

    for j in range(H_A * DH_A // LANES):
        xc = proj[:, C_QA + j * LANES:C_QA + (j + 1) * LANES]
        qa_ref[:, j * LANES:(j + 1) * LANES] = norm_rot64(xc, g64_ref[0:1, :]).astype(qa_ref.dtype)
    ka = norm_rot64(proj[:, C_KA:C_KA + LANES], g64_ref[1:2, :])
    ka_ref[...] = ka
    kab_ref[...] = ka.astype(kab_ref.dtype)
    va = proj[:, C_VA:C_VA + LANES]
    va_ref[...] = va
    vab_ref[...] = va.astype(vab_ref.dtype)
    for j in range(H_IDX * D_IDX // LANES):
        xc = proj[:, C_QI + j * LANES:C_QI + (j + 1) * LANES]
        qi_ref[:, j * LANES:(j + 1) * LANES] = _rot(xc, c16, sm16, sp16, half16).astype(qi_ref.dtype)
    xc = proj[:, C_KIWI:C_KIWI + LANES]
    lane = lax.broadcasted_iota(jnp.int32, xc.shape, 1)
    rs = lax.rsqrt(_seg_sum(xc * xc, indki_ref) * (1.0 / D_IDX) + EPS)
    kiwi = _rot(xc * jnp.where(lane < D_IDX, rs, 1.0) * gkiwi_ref[...], c16, sm16, sp16, half16)
    kiwi_ref[...] = kiwi
    kib_ref[...] = kiwi[:, :D_IDX].astype(kib_ref.dtype)
    cq = proj[:, C_CQ:C_CQ + D_CQ]
    cqn = cq * lax.rsqrt(jnp.mean(cq * cq, axis=-1, keepdims=True) + EPS) * gcq_ref[...]
    qb = jnp.dot(cqn.astype(MM_DTYPE), wuq_ref[...], preferred_element_type=F32)
    ckv = proj[:, C_CKV:C_CKV + D_C]
    ckvn = ckv * lax.rsqrt(jnp.mean(ckv * ckv, axis=-1, keepdims=True) + EPS) * gckv_ref[...]
    ckv_ref[...] = ckvn
    kpe = _rot(proj[:, C_KPE:C_KPE + LANES], c32, sm32, sp32, half32)
    kpe_ref[...] = kpe
    ckvb = ckvn.astype(MM_DTYPE)
    knope = jnp.dot(ckvb, wuk_ref[...], preferred_element_type=F32)
    vb_ref[...] = jnp.dot(ckvb, wuv_ref[...], preferred_element_type=F32).astype(vb_ref.dtype)
    for hd in range(H_B):
        sl = slice(hd * LANES, (hd + 1) * LANES)
        qh = _rot(qb[:, sl], c32, sm32, sp32, half32)
        qh = qh * lax.rsqrt(jnp.sum(qh * qh, axis=-1, keepdims=True) * (1.0 / DQK_B) + EPS) * gqb_ref[...]
        qb_ref[:, sl] = qh.astype(qb_ref.dtype)
        qbk_ref[:, sl] = (qh * gkb_ref[...]).astype(qbk_ref.dtype)
        kh = knope[:, sl] + kpe
        kh = kh * lax.rsqrt(jnp.sum(kh * kh, axis=-1, keepdims=True) * (1.0 / DQK_B) + EPS) * gkb_ref[...]
        kb_ref[:, sl] = kh.astype(kb_ref.dtype)


def rotary_tables(pos):
    p = pos.shape[0]

    def ang(n_rot):
        inv = ROPE_THETA ** (-jnp.arange(n_rot // 2, dtype=F32) * 2.0 / n_rot)
        return pos[:, None] * inv[None, :]

    a16 = ang(ROT_A)
    cos16, sin16 = jnp.cos(a16), jnp.sin(a16)
    one48 = jnp.ones((p, DH_A - ROT_A), F32)
    zero48 = jnp.zeros((p, DH_A - ROT_A), F32)
    zero8 = jnp.zeros((p, ROT_A // 2), F32)
    c16 = jnp.tile(jnp.concatenate([cos16, cos16, one48], axis=1), (1, 2))
    sm16 = jnp.tile(jnp.concatenate([-sin16, zero8, zero48], axis=1), (1, 2))
    sp16 = jnp.tile(jnp.concatenate([zero8, sin16, zero48], axis=1), (1, 2))
    a32 = ang(DR_B)
    cos32, sin32 = jnp.cos(a32), jnp.sin(a32)
    one64, zero64 = jnp.ones((p, PE_LANE), F32), jnp.zeros((p, PE_LANE), F32)
    one32, zero32 = jnp.ones((p, HEAD_PAD), F32), jnp.zeros((p, HEAD_PAD), F32)
    zero16 = jnp.zeros((p, DR_B // 2), F32)
    c32 = jnp.concatenate([one64, cos32, cos32, one32], axis=1)
    sm32 = jnp.concatenate([zero64, -sin32, zero16, zero32], axis=1)
    sp32 = jnp.concatenate([zero64, zero16, sin32, zero32], axis=1)
    return jnp.stack([c16, sm16, sp16, c32, sm32, sp32], axis=0)


def in_proj_prepare(g_norm_mix, w_in, g_q_a, g_k_a, g_k_idx, g_cq, w_uq, g_ckv, w_uk, w_uv, g_q_b, g_k_b):
    def cols(i, left=0, right=0):
        return jnp.pad(w_in[:, IN_OFFSETS[i]:IN_OFFSETS[i] + IN_WIDTHS[i]], ((0, 0), (left, right)))

    win = jnp.concatenate([cols(0), cols(1), cols(2), cols(3), cols(4, 0, WI_LANE - D_IDX),
                           cols(5, 0, LANES - WI_LANE - H_IDX), cols(6), cols(7),
                           cols(8, PE_LANE, HEAD_PAD)], axis=1).astype(MM_DTYPE)
    lane = jnp.arange(LANES)
    ind = (lane[:, None] // DH_A == lane[None, :] // DH_A).astype(MM_DTYPE)
    indki = ((lane[:, None] < D_IDX) & (lane[None, :] < D_IDX)).astype(MM_DTYPE)
    g64 = jnp.stack([jnp.tile(g_q_a, 2), jnp.tile(g_k_a, 2)], axis=0)
    gkiwi = jnp.concatenate([g_k_idx, jnp.zeros((WI_LANE - D_IDX,), F32),
                             jnp.full((H_IDX,), (H_IDX * D_IDX) ** -0.5, F32),
                             jnp.zeros((LANES - WI_LANE - H_IDX,), F32)]).reshape(1, LANES)
    wuq = jnp.pad(w_uq.reshape(D_CQ, H_B, DQK_B), ((0, 0), (0, 0), (0, HEAD_PAD))).reshape(D_CQ, H_B * LANES)
    wuk = jnp.pad(w_uk.reshape(D_C, H_B, DN_B), ((0, 0), (0, 0), (0, LANES - DN_B))).reshape(D_C, H_B * LANES)
    gqb = jnp.pad(g_q_b, (0, HEAD_PAD)).reshape(1, LANES)
    gkb = jnp.pad(g_k_b, (0, HEAD_PAD)).reshape(1, LANES)
    return (g_norm_mix.reshape(1, D_MODEL), win, ind, indki, g64, gkiwi, g_cq.reshape(1, D_CQ),
            wuq.astype(MM_DTYPE), gqb, g_ckv.reshape(1, D_C), wuk.astype(MM_DTYPE), w_uv.astype(MM_DTYPE), gkb)


def in_proj(x2d, rot, prepared, *, tile):
    t = x2d.shape[0]
    n_rot_blocks = rot.shape[1] // tile
    row = lambda i: (i, 0)
    full = lambda a: pl.BlockSpec(a.shape, lambda i: (0,) * a.ndim)
    widths = [(H_A * DH_A, MM_DTYPE), (LANES, F32), (LANES, F32), (LANES, F32), (D_C, F32), (LANES, F32),
              (LANES, MM_DTYPE), (LANES, MM_DTYPE), (D_IDX, MM_DTYPE), (H_IDX * D_IDX, MM_DTYPE),
              (H_B * LANES, MM_DTYPE), (H_B * LANES, MM_DTYPE), (H_B * DV_B, MM_DTYPE), (H_B * LANES, MM_DTYPE)]
    return pl.pallas_call(
        _in_proj_kernel,
        grid=(t // tile,),
        in_specs=[pl.BlockSpec((tile, D_MODEL), row),
                  pl.BlockSpec((6, tile, LANES), lambda i: (0, i % n_rot_blocks, 0))] + [full(a) for a in prepared],
        out_specs=[pl.BlockSpec((tile, w), row) for w, _ in widths],
        out_shape=[jax.ShapeDtypeStruct((t, w), d) for w, d in widths],
        compiler_params=pltpu.CompilerParams(dimension_semantics=("parallel",), vmem_limit_bytes=VMEM_LIMIT),
        name="in_proj",
    )(x2d, rot, *prepared)


def _sortable_key(x):
    b = pltpu.bitcast(x, jnp.int32)
    return jnp.where(b < 0, b ^ 0x7FFFFFFF, b)


def _count(mask):
    return jnp.sum(jnp.where(mask, 1.0, 0.0), axis=-1, keepdims=True)


def _kth_largest_key(key, k, two_bits):
    kf = float(k)
    t0 = jnp.where(_count(key >= 0) >= kf, 0, INT_MIN).astype(jnp.int32)

    def one_bit(i, t):
        cand = t + jnp.left_shift(jnp.int32(1), 30 - i)
        return jnp.where(_count(key >= cand) >= kf, cand, t)

    def two_bit(i, t):
        step = jnp.left_shift(jnp.int32(1), 29 - 2 * i)
        c1, c2, c3 = t + step, t + 2 * step, t + 3 * step
        n1, n2, n3 = _count(key >= c1), _count(key >= c2), _count(key >= c3)
        return jnp.where(n3 >= kf, c3, jnp.where(n2 >= kf, c2, jnp.where(n1 >= kf, c1, t)))

    if two_bits:
        return one_bit(30, lax.fori_loop(0, 15, two_bit, t0))
    return lax.fori_loop(0, 31, one_bit, t0)


def _topk_mask(score, k, tri_ref, two_bits=False):
    key = _sortable_key(score)
    t = _kth_largest_key(key, k, two_bits)
    gt = key > t
    eq = key == t
    need = float(k) - _count(gt)
    carry = jnp.zeros_like(need)
    parts = []
    for c in range(score.shape[1] // LANES):
        sl = slice(c * LANES, (c + 1) * LANES)
        eq_c = jnp.where(eq[:, sl], 1.0, 0.0)
        incl = jnp.dot(eq_c.astype(MM_DTYPE), tri_ref[...], preferred_element_type=F32)
        rank = incl - eq_c + carry
        parts.append(gt[:, sl] | (eq[:, sl] & (rank < need)))
        carry = carry + jnp.sum(eq_c, axis=-1, keepdims=True)
    return jnp.concatenate(parts, axis=1)


def _tie_rank_matrix():
    lane = jnp.arange(LANES)
    return (lane[:, None] <= lane[None, :]).astype(MM_DTYPE)


def _masked_attend(q, k, v, mask, scale):
    s = lax.dot_general(q, k, NT_DIMS, preferred_element_type=F32) * scale
    s = jnp.where(mask, s, NEG_INF)
    p = jnp.exp(s - jnp.max(s, axis=-1, keepdims=True))
    l = jnp.sum(p, axis=-1, keepdims=True)
    return jnp.dot(p.astype(MM_DTYPE), v, preferred_element_type=F32) / l


def _prompt_attn_kernel(qa_ref, qi_ref, kiwi_ref, qb_ref, ki_ref, ka_ref, va_ref, kb_ref, vb_ref, tri_ref, o_ref,
                        *, k_top, first_block):
    qn = qa_ref.shape[1]
    s_len = ki_ref.shape[1]
    q0 = (first_block + pl.program_id(1)) * qn
    qpos = q0 + lax.broadcasted_iota(jnp.int32, (qn, s_len), 0)
    kpos = lax.broadcasted_iota(jnp.int32, (qn, s_len), 1)
    causal = kpos <= qpos

    ki = ki_ref[0]
    qi = qi_ref[0]
    kiwi = kiwi_ref[0]
    score = jnp.zeros((qn, s_len), F32)
    for h in range(H_IDX):
        s = lax.dot_general(qi[:, h * D_IDX:(h + 1) * D_IDX], ki, NT_DIMS, preferred_element_type=F32)
        score = score + kiwi[:, WI_LANE + h:WI_LANE + h + 1] * jnp.maximum(s, 0.0)
    score = jnp.where(causal, score, NEG_INF)
    sel = _topk_mask(score, k_top, tri_ref) & causal

    qa = qa_ref[0]
    ka = ka_ref[0]
    va = va_ref[0]
    outs = []
    group = H_A // HKV_A
    for h in range(H_A):
        g = h // group
        outs.append(_masked_attend(qa[:, h * DH_A:(h + 1) * DH_A], ka[:, g * DH_A:(g + 1) * DH_A],
                                   va[:, g * DH_A:(g + 1) * DH_A], sel, DH_A ** -0.5))
    qb = qb_ref[0]
    for h in range(H_B):
        outs.append(_masked_attend(qb[:, h * LANES:(h + 1) * LANES], kb_ref[0, :, h * LANES:(h + 1) * LANES],
                                   vb_ref[0, :, h * DV_B:(h + 1) * DV_B], causal, DQK_B ** -0.5))
    for j in range(len(outs) // 2):
        o_ref[0, :, j * LANES:(j + 1) * LANES] = jnp.concatenate(outs[2 * j:2 * j + 2], axis=1).astype(o_ref.dtype)


def prompt_attention(qa, qi, kiwi, qb, ki, ka, va, kb, vb, *, q_block, n_classes):
    n, s_len, _ = qa.shape
    k_top = min(TOPK_MAX, s_len // 4)
    tri = _tie_rank_matrix()
    class_len = s_len // n_classes
    blocks = class_len // q_block
    outs = []
    for c in range(n_classes):
        first = c * blocks
        s_vis = (c + 1) * class_len
        qspec = lambda a: pl.BlockSpec((1, q_block, a.shape[2]), lambda b, i: (b, first + i, 0))
        kspec = lambda a: pl.BlockSpec((1, s_vis, a.shape[2]), lambda b, i: (b, 0, 0))
        outs.append(pl.pallas_call(
            functools.partial(_prompt_attn_kernel, k_top=k_top, first_block=first),
            grid=(n, blocks),
            in_specs=[qspec(qa), qspec(qi), qspec(kiwi), qspec(qb), kspec(ki), kspec(ka), kspec(va), kspec(kb),
                      kspec(vb), pl.BlockSpec(tri.shape, lambda b, i: (0, 0))],
            out_specs=pl.BlockSpec((1, q_block, D_MIX), lambda b, i: (b, i, 0)),
            out_shape=jax.ShapeDtypeStruct((n, class_len, D_MIX), MM_DTYPE),
            compiler_params=pltpu.CompilerParams(dimension_semantics=("parallel", "parallel"),
                                                 vmem_limit_bytes=VMEM_LIMIT),
            name="prompt_attention",
        )(qa, qi, kiwi, qb, ki, ka, va, kb, vb, tri))
    return jnp.concatenate(outs, axis=1)


def _page_copy(pt_ref, n, p, slot, cache, buf, sem, keys_last):
    phys = pt_ref[n * N_PAGES + p]
    keys = pl.ds(pl.multiple_of(p * PAGE_SIZE, PAGE_SIZE), PAGE_SIZE)
    lead = (slice(None),) * (len(buf.shape) - 2)
    dst = buf.at[(slot,) + lead + (keys,)] if keys_last else buf.at[slot, keys]
    return pltpu.make_async_copy(cache.at[phys], dst, sem.at[slot])


def _start_pages(pt_ref, n, slot, streams):
    def body(p, carry):
        for cache, buf, sem, keys_last in streams:
            _page_copy(pt_ref, n, p, slot, cache, buf, sem, keys_last).start()
        return carry
    lax.fori_loop(0, N_PAGES, body, 0, unroll=PAGE_LOOP_UNROLL)


def _wait_pages(pt_ref, n, slot, streams):
    def body(p, carry):
        for cache, buf, sem, keys_last in streams:
            _page_copy(pt_ref, n, p, slot, cache, buf, sem, keys_last).wait()
        return carry
    lax.fori_loop(0, N_PAGES, body, 0, unroll=PAGE_LOOP_UNROLL)


def _fetch_sequence_pages(pt_ref, streams):
    n = pl.program_id(0)
    slot = n % 2

    @pl.when(n == 0)
    def _():
        _start_pages(pt_ref, n, slot, streams)

    @pl.when(n + 1 < pl.num_programs(0))
    def _():
        _start_pages(pt_ref, n + 1, 1 - slot, streams)

    _wait_pages(pt_ref, n, slot, streams)
    return slot


def _pad_rows(a, rows):
    return jnp.concatenate([a, jnp.zeros((rows - a.shape[0], a.shape[1]), a.dtype)], axis=0)


def _new_key_mask(t_rows, reps):
    tok = lax.broadcasted_iota(jnp.int32, (reps * t_rows, LANES), 0) % t_rows
    j = lax.broadcasted_iota(jnp.int32, (reps * t_rows, LANES), 1)
    return j <= tok


def _sample_a_kernel(pt_ref, qa_ref, qi_ref, kiwi_ref, kin_ref, kan_ref, van_ref, tri_ref,
                     ckidx_hbm, ck_hbm, cv_hbm, o_ref, kidx_buf, k_buf, v_buf, sem_i, sem_k, sem_v):
    slot = _fetch_sequence_pages(pt_ref, ((ckidx_hbm, kidx_buf, sem_i, True), (ck_hbm, k_buf, sem_k, True),
                                          (cv_hbm, v_buf, sem_v, True)))
    t_new = qa_ref.shape[1]
    qi = qi_ref[0]
    kiwi = kiwi_ref[0]
    qi_rows = jnp.concatenate([qi[:, h * D_IDX:(h + 1) * D_IDX] for h in range(H_IDX)], axis=0)
    sp_all = jnp.dot(qi_rows, kidx_buf[slot].astype(MM_DTYPE), preferred_element_type=F32)
    sn_all = lax.dot_general(qi_rows, _pad_rows(kin_ref[0], LANES), NT_DIMS, preferred_element_type=F32)
    score_p = jnp.zeros((t_new, PAST_LEN), F32)
    score_n = jnp.zeros((t_new, LANES), F32)
    for h in range(H_IDX):
        wh = kiwi[:, WI_LANE + h:WI_LANE + h + 1]
        score_p = score_p + wh * jnp.maximum(sp_all[h * t_new:(h + 1) * t_new], 0.0)
        score_n = score_n + wh * jnp.maximum(sn_all[h * t_new:(h + 1) * t_new], 0.0)
    vis_n = _new_key_mask(t_new, 1)
    score = jnp.concatenate([score_p, jnp.where(vis_n, score_n, NEG_INF)], axis=1)
    k_top = min(TOPK_MAX, (PAST_LEN + t_new) // 4)
    visible = jnp.concatenate([jnp.full((t_new, PAST_LEN), True), vis_n], axis=1)
    sel = _topk_mask(score, k_top, tri_ref, two_bits=True) & visible

    group = H_A // HKV_A
    sel_g = jnp.concatenate([sel] * group, axis=0)
    qa = qa_ref[0]
    outs = []
    for g in range(HKV_A):
        gs = slice(g * DH_A, (g + 1) * DH_A)
        q_g = jnp.concatenate([qa[:, (g * group + j) * DH_A:(g * group + j + 1) * DH_A] for j in range(group)], axis=0)
        kn_g = _pad_rows(kan_ref[0][:, gs], LANES)
        vn_g = _pad_rows(van_ref[0][:, gs], LANES)
        s = jnp.concatenate([jnp.dot(q_g, k_buf[slot, g].astype(MM_DTYPE), preferred_element_type=F32),
                             lax.dot_general(q_g, kn_g, NT_DIMS, preferred_element_type=F32)], axis=1) * DH_A ** -0.5
        s = jnp.where(sel_g, s, NEG_INF)
        p = jnp.exp(s - jnp.max(s, axis=-1, keepdims=True))
        l = jnp.sum(p, axis=-1, keepdims=True)
        pb = p.astype(MM_DTYPE)
        o_g = (lax.dot_general(pb[:, :PAST_LEN], v_buf[slot, g].astype(MM_DTYPE), NT_DIMS, preferred_element_type=F32)
               + jnp.dot(pb[:, PAST_LEN:], vn_g, preferred_element_type=F32)) / l
        outs += [o_g[j * t_new:(j + 1) * t_new] for j in range(group)]
    for j in range(len(outs) // 2):
        o_ref[0, :, j * LANES:(j + 1) * LANES] = jnp.concatenate(outs[2 * j:2 * j + 2], axis=1).astype(o_ref.dtype)


def sample_attention_a(page_table, qa, qi, kiwi, ki_new, ka_new, va_new, cache_kidx, cache_k, cache_v):
    n, t_new, _ = qa.shape
    tri = _tie_rank_matrix()
    seq = lambda a: pl.BlockSpec((1, t_new, a.shape[2]), lambda i, pt: (i, 0, 0))
    hbm = pl.BlockSpec(memory_space=pl.ANY)
    return pl.pallas_call(
        _sample_a_kernel,
        grid_spec=pltpu.PrefetchScalarGridSpec(
            num_scalar_prefetch=1,
            grid=(n,),
            in_specs=[seq(qa), seq(qi), seq(kiwi), seq(ki_new), seq(ka_new), seq(va_new),
                      pl.BlockSpec(tri.shape, lambda i, pt: (0, 0)), hbm, hbm, hbm],
            out_specs=pl.BlockSpec((1, t_new, H_A * DH_A), lambda i, pt: (i, 0, 0)),
            scratch_shapes=[pltpu.VMEM((2, D_IDX, PAST_LEN), F32),
                            pltpu.VMEM((2, HKV_A, DH_A, PAST_LEN), F32),
                            pltpu.VMEM((2, HKV_A, DH_A, PAST_LEN), F32),
                            pltpu.SemaphoreType.DMA((2,)), pltpu.SemaphoreType.DMA((2,)),
                            pltpu.SemaphoreType.DMA((2,))]),
        out_shape=jax.ShapeDtypeStruct((n, t_new, H_A * DH_A), MM_DTYPE),
        compiler_params=pltpu.CompilerParams(dimension_semantics=("arbitrary",),
                                             vmem_limit_bytes=SAMPLE_VMEM_LIMIT),
        name="sample_attention_a",
    )(page_table.reshape(-1), qa, qi, kiwi, ki_new, ka_new, va_new, tri,
      jnp.transpose(cache_kidx, (0, 2, 1)), jnp.transpose(cache_k, (0, 2, 3, 1)), jnp.transpose(cache_v, (0, 2, 3, 1)))


def _sample_b_kernel(pt_ref, qb_ref, qbk_ref, kbn_ref, ckvn_ref, wukT_ref, wuv_ref,
                     ckv_hbm, kpe_hbm, o_ref, ckv_buf, kpe_buf, sem_c, sem_p):
    slot = _fetch_sequence_pages(pt_ref, ((ckv_hbm, ckv_buf, sem_c, False), (kpe_hbm, kpe_buf, sem_p, True)))
    t_new = qb_ref.shape[1]
    rows = H_B * t_new
    qb = qb_ref[0]
    qbk = qbk_ref[0]
    scale = DQK_B ** -0.5

    def update(carry, s, latent):
        m, l, acc = carry
        m_new = jnp.maximum(m, jnp.max(s, axis=-1, keepdims=True))
        alpha = jnp.exp(m - m_new)
        p = jnp.exp(s - m_new)
        l = alpha * l + jnp.sum(p, axis=-1, keepdims=True)
        acc = alpha * acc + jnp.dot(p.astype(MM_DTYPE), latent, preferred_element_type=F32)
        return m_new, l, acc

    def chunk(c, carry):
        ks = pl.ds(pl.multiple_of(c * B_CHUNK, B_CHUNK), B_CHUNK)
        ckv_c = ckv_buf[slot, ks, :].astype(MM_DTYPE)
        kpeT = kpe_buf[slot, :, ks]
        knopeT = lax.dot_general(wukT_ref[...], ckv_c, NT_DIMS, preferred_element_type=F32)
        pe_ss = jnp.sum(kpeT * kpeT, axis=0, keepdims=True)
        kpeT_b = kpeT.astype(MM_DTYPE)
        parts = []
        for h in range(H_B):
            kn = knopeT[h * DN_B:(h + 1) * DN_B]
            rs = lax.rsqrt((jnp.sum(kn * kn, axis=0, keepdims=True) + pe_ss) * (1.0 / DQK_B) + EPS)
            kT = jnp.concatenate([kn.astype(MM_DTYPE), kpeT_b], axis=0)
            parts.append(jnp.dot(qbk[:, h * LANES:h * LANES + DQK_B], kT, preferred_element_type=F32) * rs)
        return update(carry, jnp.concatenate(parts, axis=0) * scale, ckv_c)

    init = (jnp.full((rows, 1), NEG_INF, F32), jnp.zeros((rows, 1), F32), jnp.zeros((rows, D_C), F32))
    carry = lax.fori_loop(0, PAST_LEN // B_CHUNK, chunk, init, unroll=2)
    kbn = _pad_rows(kbn_ref[0], LANES)
    s_new = jnp.concatenate([lax.dot_general(qb[:, h * LANES:(h + 1) * LANES], kbn[:, h * LANES:(h + 1) * LANES],
                                             NT_DIMS, preferred_element_type=F32) for h in range(H_B)], axis=0) * scale
    s_new = jnp.where(_new_key_mask(t_new, H_B), s_new, NEG_INF)
    m, l, acc = update(carry, s_new, _pad_rows(ckvn_ref[0].astype(MM_DTYPE), LANES))
    lat = (acc / l).astype(MM_DTYPE)
    outs = [jnp.dot(lat[h * t_new:(h + 1) * t_new], wuv_ref[:, h * DV_B:(h + 1) * DV_B],
                    preferred_element_type=F32) for h in range(H_B)]
    for j in range(H_B // 2):
        o_ref[0, :, j * LANES:(j + 1) * LANES] = jnp.concatenate(outs[2 * j:2 * j + 2], axis=1).astype(o_ref.dtype)


def sample_attention_b(page_table, qb, qbk, kb_new, ckv_new, wukT, wuv, cache_ckv, cache_kpe):
    n, t_new, _ = qb.shape
    seq = lambda a: pl.BlockSpec((1, t_new, a.shape[2]), lambda i, pt: (i, 0, 0))
    full = lambda a: pl.BlockSpec(a.shape, lambda i, pt: (0,) * a.ndim)
    hbm = pl.BlockSpec(memory_space=pl.ANY)
    return pl.pallas_call(
        _sample_b_kernel,
        grid_spec=pltpu.PrefetchScalarGridSpec(
            num_scalar_prefetch=1,
            grid=(n,),
            in_specs=[seq(qb), seq(qbk), seq(kb_new), seq(ckv_new), full(wukT), full(wuv), hbm, hbm],
            out_specs=pl.BlockSpec((1, t_new, H_B * DV_B), lambda i, pt: (i, 0, 0)),
            scratch_shapes=[pltpu.VMEM((2, PAST_LEN, D_C), F32),
                            pltpu.VMEM((2, DR_B, PAST_LEN), F32),
                            pltpu.SemaphoreType.DMA((2,)), pltpu.SemaphoreType.DMA((2,))]),
        out_shape=jax.ShapeDtypeStruct((n, t_new, H_B * DV_B), MM_DTYPE),
        compiler_params=pltpu.CompilerParams(dimension_semantics=("arbitrary",),
                                             vmem_limit_bytes=SAMPLE_VMEM_LIMIT),
        name="sample_attention_b",
    )(page_table.reshape(-1), qb, qbk, kb_new, ckv_new, wukT, wuv, cache_ckv, jnp.transpose(cache_kpe, (0, 2, 1)))


LOG2E = 1.4426950408889634
GELU_A = 0.7978845608028654
GELU_B = GELU_A * 0.044715


def _sorting_network(n):
    pairs = []
    p = 1
    while p < n:
        k = p
        while k >= 1:
            for j in range(k % p, n - k, 2 * k):
                for i in range(min(k, n - j - k)):
                    if (i + j) // (2 * p) == (i + j + k) // (2 * p):
                        pairs.append((i + j, i + j + k))
            k //= 2
        p *= 2
    return pairs


def _top_values(scores, n_top):
    cols = [scores[r * SUBLANES:(r + 1) * SUBLANES, :] for r in range(scores.shape[0] // SUBLANES)]
    for i, j in _sorting_network(len(cols)):
        cols[i], cols[j] = jnp.maximum(cols[i], cols[j]), jnp.minimum(cols[i], cols[j])
    sub = lax.broadcasted_iota(jnp.int32, cols[0].shape, 0).astype(F32)
    out = []
    for k in range(n_top):
        m = jnp.max(cols[0], axis=0, keepdims=True)
        first = jnp.min(jnp.where(cols[0] == m, sub, float(SUBLANES)), axis=0, keepdims=True)
        popped = sub == first
        live = min(len(cols), n_top + 1 - k)
        for r in range(live):
            cols[r] = jnp.where(popped, cols[r + 1] if r + 1 < len(cols) else NEG_INF, cols[r])
        out.append(m)
    out.append(jnp.max(cols[0], axis=0, keepdims=True))
    return out


def _pop_max(work, row_id):
    m = jnp.max(work, axis=0, keepdims=True)
    first = jnp.min(jnp.where(work == m, row_id, float(work.shape[0])), axis=0, keepdims=True)
    return m, jnp.where(row_id == first, NEG_INF, work)


def _peer_route_kernel(x_ref, o_ref, wout_ref, g_ref, wqT_ref, subk_ref,
                       x1_ref, h2T_ref, sT_ref, thr_ref, top_ref):
    x1 = x_ref[...] + jnp.dot(o_ref[...].astype(MM_DTYPE), wout_ref[...], preferred_element_type=F32)
    x1_ref[...] = x1
    h2 = x1 * lax.rsqrt(jnp.mean(x1 * x1, axis=-1, keepdims=True) + EPS) * g_ref[...]
    h2b = h2.astype(MM_DTYPE)
    h2T_ref[...] = h2.T.astype(h2T_ref.dtype)
    qT = lax.dot_general(wqT_ref[...], h2b, NT_DIMS, preferred_element_type=F32)
    half = D_KEY // 2
    cand_id = lax.broadcasted_iota(jnp.int32, (PEER_CANDIDATES, x_ref.shape[0]), 0).astype(F32)
    for h in range(PEER_HEADS):
        s2 = []
        for p in range(2):
            r = (h * 2 + p) * half
            sT = jnp.dot(subk_ref[h * 2 + p], qT[r:r + half, :].astype(MM_DTYPE), preferred_element_type=F32) * LOG2E
            s2.append(sT)
            for k, row in enumerate(_top_values(sT, PEER_TOPK)):
                top_ref[p, k:k + 1, :] = row
        v0 = top_ref[0]
        v1 = top_ref[1]
        parts = [v0[0:1, :] + v1[0:PEER_TOPK, :]]
        for a in range(1, 8):
            parts.append(v0[a:a + 1, :] + v1[0:8, :])
        parts.append(v0[8:16, :] + v1[0:1, :])
        cand = jnp.concatenate(parts, axis=0)
        work = cand
        for k in range(PEER_TOPK):
            thr, work = _pop_max(work, cand_id)
        nxt = jnp.maximum(jnp.max(work, axis=0, keepdims=True), jnp.maximum(v0[PEER_TOPK:PEER_TOPK + 1, :] + v1[0:1, :],
                                         v0[0:1, :] + v1[PEER_TOPK:PEER_TOPK + 1, :]))
        top = v0[0:1, :] + v1[0:1, :]
        z = jnp.sum(jnp.where(cand >= thr, jnp.exp2(cand - top), 0.0), axis=0, keepdims=True)
        c = top + jnp.log2(z) + 1.0
        sT_ref[h] = s2[0] - c
        sT_ref[PEER_HEADS + h] = s2[1]
        thr_ref[h:h + 1, :] = 0.5 * (thr + nxt) - c


def peer_route(x, o, wout_b, g_ffn, wqT_b, subk_b, *, tile):
    t = x.shape[0]
    row = lambda i: (i, 0)
    col = lambda i: (0, i)
    fixed2 = lambda i: (0, 0)
    return pl.pallas_call(
        _peer_route_kernel,
        grid=(t // tile,),
        in_specs=[pl.BlockSpec((tile, D_MODEL), row),
                  pl.BlockSpec((tile, o.shape[1]), row),
                  pl.BlockSpec(wout_b.shape, fixed2),
                  pl.BlockSpec((1, D_MODEL), fixed2),
                  pl.BlockSpec(wqT_b.shape, fixed2),
                  pl.BlockSpec(subk_b.shape, lambda i: (0, 0, 0))],
        out_specs=[pl.BlockSpec((tile, D_MODEL), row),
                   pl.BlockSpec((D_MODEL, tile), col),
                   pl.BlockSpec((2 * PEER_HEADS, N_KEYS, tile), lambda i: (0, 0, i)),
                   pl.BlockSpec((PEER_HEADS, tile), col)],
        out_shape=[jax.ShapeDtypeStruct((t, D_MODEL), F32),
                   jax.ShapeDtypeStruct((D_MODEL, t), MM_DTYPE),
                   jax.ShapeDtypeStruct((2 * PEER_HEADS, N_KEYS, t), F32),
                   jax.ShapeDtypeStruct((PEER_HEADS, t), F32)],
        scratch_shapes=[pltpu.VMEM((2, PEER_TOPK + 8, tile), F32)],
        compiler_params=pltpu.CompilerParams(dimension_semantics=("parallel",), vmem_limit_bytes=VMEM_LIMIT),
        name="peer_route",
    )(x, o, wout_b, g_ffn.reshape(1, D_MODEL), wqT_b, subk_b)


def _peer_dense_kernel(h2T_ref, u_ref, vT_ref, s0_ref, s1_ref, thr_ref, x1_ref, y_ref, acc_ref):
    ei = pl.program_id(1)
    rows_per_step = s0_ref.shape[1]

    @pl.when(ei == 0)
    def _():
        acc_ref[...] = jnp.zeros_like(acc_ref)

    hT = jnp.dot(u_ref[...], h2T_ref[...], preferred_element_type=F32)
    ws = []
    for ii in range(rows_per_step):
        hid = hT[ii * N_KEYS:(ii + 1) * N_KEYS, :]
        gate = jnp.zeros_like(hid)
        for h in range(PEER_HEADS):
            arg = s0_ref[h, ii:ii + 1, :] + s1_ref[h]
            gate = gate + jnp.where(arg >= thr_ref[h:h + 1, :], jnp.exp2(arg), 0.0)
        gx = gate * hid
        ws.append((gx + gx * jnp.tanh(hid * (GELU_A + GELU_B * (hid * hid)))).astype(MM_DTYPE))
    w = jnp.concatenate(ws, axis=0)
    acc_ref[...] += jnp.dot(vT_ref[...], w, preferred_element_type=F32)

    @pl.when(ei == pl.num_programs(1) - 1)
    def _():
        y_ref[...] = x1_ref[...] + acc_ref[...].T


def peer_dense(h2T, u_b, vT_b, sT, thr, x1, *, tile, etile):
    t = x1.shape[0]
    n_exp = u_b.shape[0]
    rows = etile // N_KEYS
    return pl.pallas_call(
        _peer_dense_kernel,
        grid=(t // tile, n_exp // etile),
        in_specs=[pl.BlockSpec((D_MODEL, tile), lambda ti, ei: (0, ti)),
                  pl.BlockSpec((etile, D_MODEL), lambda ti, ei: (ei, 0)),
                  pl.BlockSpec((D_MODEL, etile), lambda ti, ei: (0, ei)),
                  pl.BlockSpec((PEER_HEADS, rows, tile), lambda ti, ei: (0, ei, ti)),
                  pl.BlockSpec((PEER_HEADS, N_KEYS, tile), lambda ti, ei: (1, 0, ti)),
                  pl.BlockSpec((PEER_HEADS, tile), lambda ti, ei: (0, ti)),
                  pl.BlockSpec((tile, D_MODEL), lambda ti, ei: (ti, 0))],
        out_specs=pl.BlockSpec((tile, D_MODEL), lambda ti, ei: (ti, 0)),
        out_shape=jax.ShapeDtypeStruct((t, D_MODEL), F32),
        scratch_shapes=[pltpu.VMEM((D_MODEL, tile), F32)],
        compiler_params=pltpu.CompilerParams(dimension_semantics=("parallel", "arbitrary"),
                                             vmem_limit_bytes=VMEM_LIMIT),
        name="peer_dense",
    )(h2T, u_b, vT_b, sT, sT, thr, x1)


def peer_prepare(w_out, w_peer_q, peer_sub_keys, peer_u, peer_v):
    return (w_out.astype(MM_DTYPE), w_peer_q.T.astype(MM_DTYPE),
            peer_sub_keys.reshape(2 * PEER_HEADS, N_KEYS, D_KEY // 2).astype(MM_DTYPE),
            peer_u.astype(MM_DTYPE), peer_v.T.astype(MM_DTYPE))


def peer_layer(x2d, o2d, g_ffn, prepared):
    wout_b, wqT_b, subk_b, u_b, vT_b = prepared
    x1, h2T, sT, thr = peer_route(x2d, o2d, wout_b, g_ffn, wqT_b, subk_b, tile=ROUTE_TILE)
    return peer_dense(h2T, u_b, vT_b, sT, thr, x1, tile=PEER_TOKEN_TILE, etile=PEER_EXPERT_TILE)


def kernel(x_prompt, x_sample, cache_a_k, cache_a_v, cache_a_kidx, cache_b_ckv, cache_b_kpe, page_table,
           g_norm_mix, w_in, g_q_a, g_k_a, g_k_idx, g_cq, w_uq, g_ckv, w_uk, w_uv, g_q_b, g_k_b,
           w_out, g_norm_ffn, w_peer_q, peer_sub_keys, peer_u, peer_v):
    n_p, s_len, _ = x_prompt.shape
    n_s, t_new, _ = x_sample.shape
    in_prep = in_proj_prepare(g_norm_mix, w_in, g_q_a, g_k_a, g_k_idx, g_cq, w_uq, g_ckv, w_uk, w_uv, g_q_b, g_k_b)
    wuv_b = in_prep[11]
    peer_prep = peer_prepare(w_out, w_peer_q, peer_sub_keys, peer_u, peer_v)

    rot_p = rotary_tables(jnp.arange(s_len, dtype=F32))
    (_qa, ka, va, kiwi, ckv, kpe, ka_b, va_b, ki_b, qi, qb, kb, vb, _) = in_proj(
        x_prompt.reshape(-1, D_MODEL), rot_p, in_prep, tile=IN_TILE)
    seq_p = lambda a: a.reshape(n_p, s_len, a.shape[-1])
    o_p = prompt_attention(seq_p(_qa), seq_p(qi), seq_p(kiwi), seq_p(qb), seq_p(ki_b), seq_p(ka_b), seq_p(va_b),
                           seq_p(kb), seq_p(vb), q_block=Q_BLOCK, n_classes=CAUSAL_CLASSES)
    yp = peer_layer(x_prompt.reshape(-1, D_MODEL), o_p.reshape(-1, D_MIX), g_norm_ffn, peer_prep)
    prompt_rows = (ka.reshape(n_p, s_len, HKV_A, DH_A), va.reshape(n_p, s_len, HKV_A, DH_A),
                   kiwi[:, :D_IDX].reshape(n_p, s_len, D_IDX), ckv.reshape(n_p, s_len, D_C),
                   kpe[:, PE_LANE:PE_LANE + DR_B].reshape(n_p, s_len, DR_B))

    rot_s = rotary_tables(jnp.tile(PAST_LEN + jnp.arange(t_new, dtype=F32), IN_TILE // t_new))
    (_qa, ka, va, kiwi, ckv, kpe, ka_b, va_b, ki_b, qi, qb, kb, vb, qbk) = in_proj(
        x_sample.reshape(-1, D_MODEL), rot_s, in_prep, tile=IN_TILE)
    seq_s = lambda a: a.reshape(n_s, t_new, a.shape[-1])
    o_a = sample_attention_a(page_table, seq_s(_qa), seq_s(qi), seq_s(kiwi), seq_s(ki_b), seq_s(ka_b), seq_s(va_b),
                             cache_a_kidx, cache_a_k, cache_a_v)
    o_b = sample_attention_b(page_table, seq_s(qb), seq_s(qbk), seq_s(kb), seq_s(ckv), w_uk.T.astype(MM_DTYPE), wuv_b,
                             cache_b_ckv, cache_b_kpe)
    o_s = jnp.concatenate([o_a, o_b], axis=-1)
    ys = peer_layer(x_sample.reshape(-1, D_MODEL), o_s.reshape(-1, D_MIX), g_norm_ffn, peer_prep)
    sample_rows = (ka.reshape(n_s, t_new, HKV_A, DH_A), va.reshape(n_s, t_new, HKV_A, DH_A),
                   kiwi[:, :D_IDX].reshape(n_s, t_new, D_IDX), ckv.reshape(n_s, t_new, D_C),
                   kpe[:, PE_LANE:PE_LANE + DR_B].reshape(n_s, t_new, DR_B))
    return (yp.reshape(x_prompt.shape), ys.reshape(x_sample.shape)) + prompt_rows + sample_rows
```

```python
import functools

import jax
import jax.numpy as jnp
from jax import lax
from jax.experimental import pallas as pl
from jax.experimental.pallas import tpu as pltpu

D_MODEL = 1024
PAST_LEN = 8192
PAGE_SIZE = 128
N_PAGES = PAST_LEN // PAGE_SIZE
ROPE_THETA = 500000.0
EPS = 1e-6
H_A = 8
HKV_A = 2
DH_A = 64
ROT_A = DH_A // 4
H_IDX = 4
D_IDX = 64
TOPK_MAX = 256
H_B = 8
DN_B = 64
DR_B = 32
DV_B = 64
D_CQ = 384
D_C = 256
PEER_HEADS = 8
N_KEYS = 128
D_KEY = 128
PEER_TOPK = 16
IN_WIDTHS = (H_A * DH_A, HKV_A * DH_A, HKV_A * DH_A, H_IDX * D_IDX, D_IDX, H_IDX, D_CQ, D_C, DR_B)
IN_OFFSETS = tuple(sum(IN_WIDTHS[:i]) for i in range(len(IN_WIDTHS)))
D_MIX = H_A * DH_A + H_B * DV_B

MM_DTYPE = jnp.bfloat16
F32 = jnp.float32
NEG_INF = float("-inf")
INT_MIN = -2 ** 31
LANES = 128
SUBLANES = 8
VMEM_LIMIT = 48 * 1024 * 1024
SAMPLE_VMEM_LIMIT = 56 * 1024 * 1024
NT_DIMS = (((1,), (1,)), ((), ()))

C_QA, C_KA, C_VA, C_QI, C_KIWI, C_CQ, C_CKV, C_KPE, W_IN = 0, 512, 640, 768, 1024, 1152, 1536, 1792, 1920
WI_LANE = 80
PE_LANE = DN_B
DQK_B = DN_B + DR_B
HEAD_PAD = LANES - DQK_B
SCALE_A = DH_A ** -0.5
SCALE_B = DQK_B ** -0.5

IN_TILE = 256
ROUTE_TILE = 256
PEER_TOKEN_TILE = 512
PEER_EXPERT_TILE = 1024
Q_BLOCK = 128
CAUSAL_CLASSES = 8
B_CHUNK = 1024
PAGE_LOOP_UNROLL = 8
PEER_CANDIDATES = 80


def _seg_sum(sq, ind_ref):
    hi = sq.astype(MM_DTYPE)
    lo = (sq - hi.astype(F32)).astype(MM_DTYPE)
    ind = ind_ref[...]
    return jnp.dot(hi, ind, preferred_element_type=F32) + jnp.dot(lo, ind, preferred_element_type=F32)


def _rot(x, c, sm, sp, half):
    return x * c + pltpu.roll(x, LANES - half, 1) * sm + pltpu.roll(x, half, 1) * sp


def _in_proj_kernel(x_ref, rot_ref, gmix_ref, win_ref, ind_ref, indki_ref, g64_ref, gkiwi_ref, gcq_ref, wuq_ref,
                    gqb_ref, gckv_ref, wuk_ref, wuv_ref, gkb_ref,
                    qa_ref, ka_ref, va_ref, kiwi_ref, ckv_ref, kpe_ref,
                    kab_ref, vab_ref, kib_ref, qi_ref, qb_ref, kb_ref, vb_ref, qbk_ref):
    x = x_ref[...]
    h = x * lax.rsqrt(jnp.mean(x * x, axis=-1, keepdims=True) + EPS) * gmix_ref[...]
    proj = jnp.dot(h.astype(MM_DTYPE), win_ref[...], preferred_element_type=F32)
    c16, sm16, sp16 = rot_ref[0], rot_ref[1], rot_ref[2]
    c32, sm32, sp32 = rot_ref[3], rot_ref[4], rot_ref[5]
    half16, half32 = ROT_A // 2, DR_B // 2

    def norm_rot64(xc, g):
        rs = lax.rsqrt(_seg_sum(xc * xc, ind_ref) * (1.0 / DH_A) + EPS)
        return _rot(xc * rs * g, c16, sm16, sp16, half16)

    for j in range(H_A * DH_A // LANES):
        xc = proj[:, C_QA + j * LANES:C_QA + (j + 1) * LANES]
        qa_ref[:, j * LANES:(j + 1) * LANES] = (norm_rot64(xc, g64_ref[0:1, :]) * SCALE_A).astype(qa_ref.dtype)
    ka = norm_rot64(proj[:, C_KA:C_KA + LANES], g64_ref[1:2, :])
    ka_ref[...] = ka
    kab_ref[...] = ka.astype(kab_ref.dtype)
    va = proj[:, C_VA:C_VA + LANES]
    va_ref[...] = va
    vab_ref[...] = va.astype(vab_ref.dtype)
    for j in range(H_IDX * D_IDX // LANES):
        xc = proj[:, C_QI + j * LANES:C_QI + (j + 1) * LANES]
        qi_ref[:, j * LANES:(j + 1) * LANES] = _rot(xc, c16, sm16, sp16, half16).astype(qi_ref.dtype)
    xc = proj[:, C_KIWI:C_KIWI + LANES]
    lane = lax.broadcasted_iota(jnp.int32, xc.shape, 1)
    rs = lax.rsqrt(_seg_sum(xc * xc, indki_ref) * (1.0 / D_IDX) + EPS)
    kiwi = _rot(xc * jnp.where(lane < D_IDX, rs, 1.0) * gkiwi_ref[...], c16, sm16, sp16, half16)
    kiwi_ref[...] = kiwi
    kib_ref[...] = kiwi[:, :D_IDX].astype(kib_ref.dtype)
    cq = proj[:, C_CQ:C_CQ + D_CQ]
    cqn = cq * lax.rsqrt(jnp.mean(cq * cq, axis=-1, keepdims=True) + EPS) * gcq_ref[...]
    qb = jnp.dot(cqn.astype(MM_DTYPE), wuq_ref[...], preferred_element_type=F32)
    ckv = proj[:, C_CKV:C_CKV + D_C]
    ckvn = ckv * lax.rsqrt(jnp.mean(ckv * ckv, axis=-1, keepdims=True) + EPS) * gckv_ref[...]
    ckv_ref[...] = ckvn
    kpe = _rot(proj[:, C_KPE:C_KPE + LANES], c32, sm32, sp32, half32)
    kpe_ref[...] = kpe
    ckvb = ckvn.astype(MM_DTYPE)
    knope = jnp.dot(ckvb, wuk_ref[...], preferred_element_type=F32)
    vb_ref[...] = jnp.dot(ckvb, wuv_ref[...], preferred_element_type=F32).astype(vb_ref.dtype)
    for hd in range(H_B):
        sl = slice(hd * LANES, (hd + 1) * LANES)
        qh = _rot(qb[:, sl], c32, sm32, sp32, half32)
        qh = qh * lax.rsqrt(jnp.sum(qh * qh, axis=-1, keepdims=True) * (1.0 / DQK_B) + EPS) * (gqb_ref[...] * SCALE_B)
        qb_ref[:, sl] = qh.astype(qb_ref.dtype)
        qbk_ref[:, sl] = (qh * gkb_ref[...]).astype(qbk_ref.dtype)
        kh = knope[:, sl] + kpe
        kh = kh * lax.rsqrt(jnp.sum(kh * kh, axis=-1, keepdims=True) * (1.0 / DQK_B) + EPS) * gkb_ref[...]
        kb_ref[:, sl] = kh.astype(kb_ref.dtype)


def rotary_tables(pos):
    p = pos.shape[0]

    def ang(n_rot):
        inv = ROPE_THETA ** (-jnp.arange(n_rot // 2, dtype=F32) * 2.0 / n_rot)
        return pos[:, None] * inv[None, :]

    a16 = ang(ROT_A)
    cos16, sin16 = jnp.cos(a16), jnp.sin(a16)
    one48 = jnp.ones((p, DH_A - ROT_A), F32)
    zero48 = jnp.zeros((p, DH_A - ROT_A), F32)
    zero8 = jnp.zeros((p, ROT_A // 2), F32)
    c16 = jnp.tile(jnp.concatenate([cos16, cos16, one48], axis=1), (1, 2))
    sm16 = jnp.tile(jnp.concatenate([-sin16, zero8, zero48], axis=1), (1, 2))
    sp16 = jnp.tile(jnp.concatenate([zero8, sin16, zero48], axis=1), (1, 2))
    a32 = ang(DR_B)
    cos32, sin32 = jnp.cos(a32), jnp.sin(a32)
    one64, zero64 = jnp.ones((p, PE_LANE), F32), jnp.zeros((p, PE_LANE), F32)
    one32, zero32 = jnp.ones((p, HEAD_PAD), F32), jnp.zeros((p, HEAD_PAD), F32)
    zero16 = jnp.zeros((p, DR_B // 2), F32)
    c32 = jnp.concatenate([one64, cos32, cos32, one32], axis=1)
    sm32 = jnp.concatenate([zero64, -sin32, zero16, zero32], axis=1)
    sp32 = jnp.concatenate([zero64, zero16, sin32, zero32], axis=1)
    return jnp.stack([c16, sm16, sp16, c32, sm32, sp32], axis=0)


def in_proj_prepare(g_norm_mix, w_in, g_q_a, g_k_a, g_k_idx, g_cq, w_uq, g_ckv, w_uk, w_uv, g_q_b, g_k_b):
    def cols(i, left=0, right=0):
        return jnp.pad(w_in[:, IN_OFFSETS[i]:IN_OFFSETS[i] + IN_WIDTHS[i]], ((0, 0), (left, right)))

    win = jnp.concatenate([cols(0), cols(1), cols(2), cols(3), cols(4, 0, WI_LANE - D_IDX),
                           cols(5, 0, LANES - WI_LANE - H_IDX), cols(6), cols(7),
                           cols(8, PE_LANE, HEAD_PAD)], axis=1).astype(MM_DTYPE)
    lane = jnp.arange(LANES)
    ind = (lane[:, None] // DH_A == lane[None, :] // DH_A).astype(MM_DTYPE)
    indki = ((lane[:, None] < D_IDX) & (lane[None, :] < D_IDX)).astype(MM_DTYPE)
    g64 = jnp.stack([jnp.tile(g_q_a, 2), jnp.tile(g_k_a, 2)], axis=0)
    gkiwi = jnp.concatenate([g_k_idx, jnp.zeros((WI_LANE - D_IDX,), F32),
                             jnp.full((H_IDX,), (H_IDX * D_IDX) ** -0.5, F32),
                             jnp.zeros((LANES - WI_LANE - H_IDX,), F32)]).reshape(1, LANES)
    wuq = jnp.pad(w_uq.reshape(D_CQ, H_B, DQK_B), ((0, 0), (0, 0), (0, HEAD_PAD))).reshape(D_CQ, H_B * LANES)
    wuk = jnp.pad(w_uk.reshape(D_C, H_B, DN_B), ((0, 0), (0, 0), (0, LANES - DN_B))).reshape(D_C, H_B * LANES)
    gqb = jnp.pad(g_q_b, (0, HEAD_PAD)).reshape(1, LANES)
    gkb = jnp.pad(g_k_b, (0, HEAD_PAD)).reshape(1, LANES)
    return (g_norm_mix.reshape(1, D_MODEL), win, ind, indki, g64, gkiwi, g_cq.reshape(1, D_CQ),
            wuq.astype(MM_DTYPE), gqb, g_ckv.reshape(1, D_C), wuk.astype(MM_DTYPE), w_uv.astype(MM_DTYPE), gkb)


def in_proj(x2d, rot, prepared, *, tile):
    t = x2d.shape[0]
    n_rot_blocks = rot.shape[1] // tile
    row = lambda i: (i, 0)
    full = lambda a: pl.BlockSpec(a.shape, lambda i: (0,) * a.ndim)
    widths = [(H_A * DH_A, MM_DTYPE), (LANES, F32), (LANES, F32), (LANES, F32), (D_C, F32), (LANES, F32),
              (LANES, MM_DTYPE), (LANES, MM_DTYPE), (D_IDX, MM_DTYPE), (H_IDX * D_IDX, MM_DTYPE),
              (H_B * LANES, MM_DTYPE), (H_B * LANES, MM_DTYPE), (H_B * DV_B, MM_DTYPE), (H_B * LANES, MM_DTYPE)]
    return pl.pallas_call(
        _in_proj_kernel,
        grid=(t // tile,),
        in_specs=[pl.BlockSpec((tile, D_MODEL), row),
                  pl.BlockSpec((6, tile, LANES), lambda i: (0, i % n_rot_blocks, 0))] + [full(a) for a in prepared],
        out_specs=[pl.BlockSpec((tile, w), row) for w, _ in widths],
        out_shape=[jax.ShapeDtypeStruct((t, w), d) for w, d in widths],
        compiler_params=pltpu.CompilerParams(dimension_semantics=("parallel",), vmem_limit_bytes=VMEM_LIMIT),
        name="in_proj",
    )(x2d, rot, *prepared)


def _sortable_key(x):
    b = pltpu.bitcast(x, jnp.int32)
    return jnp.where(b < 0, b ^ 0x7FFFFFFF, b)


def _count(mask):
    return jnp.sum(jnp.where(mask, 1.0, 0.0), axis=-1, keepdims=True)


def _kth_largest_key(key, k, two_bits):
    kf = float(k)
    t0 = jnp.where(_count(key >= 0) >= kf, 0, INT_MIN).astype(jnp.int32)

    def one_bit(i, t):
        cand = t + jnp.left_shift(jnp.int32(1), 30 - i)
        return jnp.where(_count(key >= cand) >= kf, cand, t)

    def two_bit(i, t):
        step = jnp.left_shift(jnp.int32(1), 29 - 2 * i)
        c1, c2, c3 = t + step, t + 2 * step, t + 3 * step
        n1, n2, n3 = _count(key >= c1), _count(key >= c2), _count(key >= c3)
        return jnp.where(n3 >= kf, c3, jnp.where(n2 >= kf, c2, jnp.where(n1 >= kf, c1, t)))

    if two_bits:
        return one_bit(30, lax.fori_loop(0, 15, two_bit, t0))
    return lax.fori_loop(0, 31, one_bit, t0)


def _topk_mask(score, k, tri_ref, mask_ref, two_bits=False):
    key = _sortable_key(score)
    t = _kth_largest_key(key, k, two_bits)
    ge = key >= t
    mask_ref[...] = jnp.where(ge, 1.0, 0.0)

    @pl.when(jnp.max(_count(ge)) > float(k))
    def _():
        gt = key > t
        eq = key == t
        need = float(k) - _count(gt)
        carry = jnp.zeros_like(need)
        for c in range(score.shape[1] // LANES):
            sl = slice(c * LANES, (c + 1) * LANES)
            eq_c = jnp.where(eq[:, sl], 1.0, 0.0)
            incl = jnp.dot(eq_c.astype(MM_DTYPE), tri_ref[...], preferred_element_type=F32)
            rank = incl - eq_c + carry
            mask_ref[:, sl] = jnp.where(gt[:, sl] | (eq[:, sl] & (rank < need)), 1.0, 0.0)
            carry = carry + jnp.sum(eq_c, axis=-1, keepdims=True)

    return mask_ref[...] > 0.5


def _tie_rank_matrix():
    lane = jnp.arange(LANES)
    return (lane[:, None] <= lane[None, :]).astype(MM_DTYPE)


def _masked_attend(q, k, v, mask):
    s = lax.dot_general(q, k, NT_DIMS, preferred_element_type=F32)
    s = jnp.where(mask, s, NEG_INF)
    p = jnp.exp(s - jnp.max(s, axis=-1, keepdims=True))
    l = jnp.sum(p, axis=-1, keepdims=True)
    return jnp.dot(p.astype(MM_DTYPE), v, preferred_element_type=F32) / l


def _prompt_attn_kernel(qa_ref, qi_ref, kiwi_ref, qb_ref, ki_ref, ka_ref, va_ref, kb_ref, vb_ref, tri_ref, o_ref,
                        mask_ref, *, k_top, first_block):
    qn = qa_ref.shape[1]
    s_len = ki_ref.shape[1]
    q0 = (first_block + pl.program_id(1)) * qn
    qpos = q0 + lax.broadcasted_iota(jnp.int32, (qn, s_len), 0)
    kpos = lax.broadcasted_iota(jnp.int32, (qn, s_len), 1)
    causal = kpos <= qpos

    ki = ki_ref[0]
    qi = qi_ref[0]
    kiwi = kiwi_ref[0]
    score = jnp.zeros((qn, s_len), F32)
    for h in range(H_IDX):
        s = lax.dot_general(qi[:, h * D_IDX:(h + 1) * D_IDX], ki, NT_DIMS, preferred_element_type=F32)
        score = score + kiwi[:, WI_LANE + h:WI_LANE + h + 1] * jnp.maximum(s, 0.0)
    score = jnp.where(causal, score, NEG_INF)
    sel = _topk_mask(score, k_top, tri_ref, mask_ref) & causal

    qa = qa_ref[0]
    ka = ka_ref[0]
    va = va_ref[0]
    outs = []
    group = H_A // HKV_A
    for h in range(H_A):
        g = h // group
        outs.append(_masked_attend(qa[:, h * DH_A:(h + 1) * DH_A], ka[:, g * DH_A:(g + 1) * DH_A],
                                   va[:, g * DH_A:(g + 1) * DH_A], sel))
    qb = qb_ref[0]
    for h in range(H_B):
        outs.append(_masked_attend(qb[:, h * LANES:(h + 1) * LANES], kb_ref[0, :, h * LANES:(h + 1) * LANES],
                                   vb_ref[0, :, h * DV_B:(h + 1) * DV_B], causal))
    for j in range(len(outs) // 2):
        o_ref[0, :, j * LANES:(j + 1) * LANES] = jnp.concatenate(outs[2 * j:2 * j + 2], axis=1).astype(o_ref.dtype)


def prompt_attention(qa, qi, kiwi, qb, ki, ka, va, kb, vb, *, q_block, n_classes):
    n, s_len, _ = qa.shape
    k_top = min(TOPK_MAX, s_len // 4)
    tri = _tie_rank_matrix()
    class_len = s_len // n_classes
    blocks = class_len // q_block
    outs = []
    for c in range(n_classes):
        first = c * blocks
        s_vis = (c + 1) * class_len
        qspec = lambda a: pl.BlockSpec((1, q_block, a.shape[2]), lambda b, i: (b, first + i, 0))
        kspec = lambda a: pl.BlockSpec((1, s_vis, a.shape[2]), lambda b, i: (b, 0, 0))
        outs.append(pl.pallas_call(
            functools.partial(_prompt_attn_kernel, k_top=k_top, first_block=first),
            grid=(n, blocks),
            in_specs=[qspec(qa), qspec(qi), qspec(kiwi), qspec(qb), kspec(ki), kspec(ka), kspec(va), kspec(kb),
                      kspec(vb), pl.BlockSpec(tri.shape, lambda b, i: (0, 0))],
            out_specs=pl.BlockSpec((1, q_block, D_MIX), lambda b, i: (b, i, 0)),
            out_shape=jax.ShapeDtypeStruct((n, class_len, D_MIX), MM_DTYPE),
            scratch_shapes=[pltpu.VMEM((q_block, s_vis), F32)],
            compiler_params=pltpu.CompilerParams(dimension_semantics=("parallel", "parallel"),
                                                 vmem_limit_bytes=VMEM_LIMIT),
            name="prompt_attention",
        )(qa, qi, kiwi, qb, ki, ka, va, kb, vb, tri))
    return jnp.concatenate(outs, axis=1)


def _page_copy(pt_ref, n, p, slot, cache, buf, sem, keys_last):
    phys = pt_ref[n * N_PAGES + p]
    keys = pl.ds(pl.multiple_of(p * PAGE_SIZE, PAGE_SIZE), PAGE_SIZE)
    lead = (slice(None),) * (len(buf.shape) - 2)
    dst = buf.at[(slot,) + lead + (keys,)] if keys_last else buf.at[slot, keys]
    return pltpu.make_async_copy(cache.at[phys], dst, sem.at[slot])


def _start_pages(pt_ref, n, slot, streams):
    def body(p, carry):
        for cache, buf, sem, keys_last in streams:
            _page_copy(pt_ref, n, p, slot, cache, buf, sem, keys_last).start()
        return carry
    lax.fori_loop(0, N_PAGES, body, 0, unroll=PAGE_LOOP_UNROLL)


def _wait_pages(pt_ref, n, slot, streams):
    def body(p, carry):
        for cache, buf, sem, keys_last in streams:
            _page_copy(pt_ref, n, p, slot, cache, buf, sem, keys_last).wait()
        return carry
    lax.fori_loop(0, N_PAGES, body, 0, unroll=PAGE_LOOP_UNROLL)


def _fetch_sequence_pages(pt_ref, streams):
    n = pl.program_id(0)
    slot = n % 2

    @pl.when(n == 0)
    def _():
        _start_pages(pt_ref, n, slot, streams)

    @pl.when(n + 1 < pl.num_programs(0))
    def _():
        _start_pages(pt_ref, n + 1, 1 - slot, streams)

    _wait_pages(pt_ref, n, slot, streams)
    return slot


def _pad_rows(a, rows):
    return jnp.concatenate([a, jnp.zeros((rows - a.shape[0], a.shape[1]), a.dtype)], axis=0)


def _new_key_mask(t_rows, reps):
    tok = lax.broadcasted_iota(jnp.int32, (reps * t_rows, LANES), 0) % t_rows
    j = lax.broadcasted_iota(jnp.int32, (reps * t_rows, LANES), 1)
    return j <= tok


def _sample_a_kernel(pt_ref, qa_ref, qi_ref, kiwi_ref, kin_ref, kan_ref, van_ref, tri_ref,
                     ckidx_hbm, ck_hbm, cv_hbm, o_ref, kidx_buf, k_buf, v_buf, sem_i, sem_k, sem_v, mask_ref):
    slot = _fetch_sequence_pages(pt_ref, ((ckidx_hbm, kidx_buf, sem_i, True), (ck_hbm, k_buf, sem_k, True),
                                          (cv_hbm, v_buf, sem_v, True)))
    t_new = qa_ref.shape[1]
    qi = qi_ref[0]
    kiwi = kiwi_ref[0]
    qi_rows = jnp.concatenate([qi[:, h * D_IDX:(h + 1) * D_IDX] for h in range(H_IDX)], axis=0)
    sp_all = jnp.dot(qi_rows, kidx_buf[slot].astype(MM_DTYPE), preferred_element_type=F32)
    sn_all = lax.dot_general(qi_rows, _pad_rows(kin_ref[0], LANES), NT_DIMS, preferred_element_type=F32)
    score_p = jnp.zeros((t_new, PAST_LEN), F32)
    score_n = jnp.zeros((t_new, LANES), F32)
    for h in range(H_IDX):
        wh = kiwi[:, WI_LANE + h:WI_LANE + h + 1]
        score_p = score_p + wh * jnp.maximum(sp_all[h * t_new:(h + 1) * t_new], 0.0)
        score_n = score_n + wh * jnp.maximum(sn_all[h * t_new:(h + 1) * t_new], 0.0)
    vis_n = _new_key_mask(t_new, 1)
    score = jnp.concatenate([score_p, jnp.where(vis_n, score_n, NEG_INF)], axis=1)
    k_top = min(TOPK_MAX, (PAST_LEN + t_new) // 4)
    visible = jnp.concatenate([jnp.full((t_new, PAST_LEN), True), vis_n], axis=1)
    sel = _topk_mask(score, k_top, tri_ref, mask_ref, two_bits=True) & visible

    group = H_A // HKV_A
    sel_g = jnp.concatenate([sel] * group, axis=0)
    qa = qa_ref[0]
    outs = []
    for g in range(HKV_A):
        gs = slice(g * DH_A, (g + 1) * DH_A)
        q_g = jnp.concatenate([qa[:, (g * group + j) * DH_A:(g * group + j + 1) * DH_A] for j in range(group)], axis=0)
        kn_g = _pad_rows(kan_ref[0][:, gs], LANES)
        vn_g = _pad_rows(van_ref[0][:, gs], LANES)
        s = jnp.concatenate([jnp.dot(q_g, k_buf[slot, g].astype(MM_DTYPE), preferred_element_type=F32),
                             lax.dot_general(q_g, kn_g, NT_DIMS, preferred_element_type=F32)], axis=1)
        s = jnp.where(sel_g, s, NEG_INF)
        p = jnp.exp(s - jnp.max(s, axis=-1, keepdims=True))
        l = jnp.sum(p, axis=-1, keepdims=True)
        pb = p.astype(MM_DTYPE)
        o_g = (lax.dot_general(pb[:, :PAST_LEN], v_buf[slot, g].astype(MM_DTYPE), NT_DIMS, preferred_element_type=F32)
               + jnp.dot(pb[:, PAST_LEN:], vn_g, preferred_element_type=F32)) / l
        outs += [o_g[j * t_new:(j + 1) * t_new] for j in range(group)]
    for j in range(len(outs) // 2):
        o_ref[0, :, j * LANES:(j + 1) * LANES] = jnp.concatenate(outs[2 * j:2 * j + 2], axis=1).astype(o_ref.dtype)


def sample_attention_a(page_table, qa, qi, kiwi, ki_new, ka_new, va_new, cache_kidx, cache_k, cache_v):
    n, t_new, _ = qa.shape
    tri = _tie_rank_matrix()
    seq = lambda a: pl.BlockSpec((1, t_new, a.shape[2]), lambda i, pt: (i, 0, 0))
    hbm = pl.BlockSpec(memory_space=pl.ANY)
    return pl.pallas_call(
        _sample_a_kernel,
        grid_spec=pltpu.PrefetchScalarGridSpec(
            num_scalar_prefetch=1,
            grid=(n,),
            in_specs=[seq(qa), seq(qi), seq(kiwi), seq(ki_new), seq(ka_new), seq(va_new),
                      pl.BlockSpec(tri.shape, lambda i, pt: (0, 0)), hbm, hbm, hbm],
            out_specs=pl.BlockSpec((1, t_new, H_A * DH_A), lambda i, pt: (i, 0, 0)),
            scratch_shapes=[pltpu.VMEM((2, D_IDX, PAST_LEN), F32),
                            pltpu.VMEM((2, HKV_A, DH_A, PAST_LEN), F32),
                            pltpu.VMEM((2, HKV_A, DH_A, PAST_LEN), F32),
                            pltpu.SemaphoreType.DMA((2,)), pltpu.SemaphoreType.DMA((2,)),
                            pltpu.SemaphoreType.DMA((2,)), pltpu.VMEM((t_new, PAST_LEN + LANES), F32)]),
        out_shape=jax.ShapeDtypeStruct((n, t_new, H_A * DH_A), MM_DTYPE),
        compiler_params=pltpu.CompilerParams(dimension_semantics=("arbitrary",),
                                             vmem_limit_bytes=SAMPLE_VMEM_LIMIT),
        name="sample_attention_a",
    )(page_table.reshape(-1), qa, qi, kiwi, ki_new, ka_new, va_new, tri,
      jnp.transpose(cache_kidx, (0, 2, 1)), jnp.transpose(cache_k, (0, 2, 3, 1)), jnp.transpose(cache_v, (0, 2, 3, 1)))


def _sample_b_kernel(pt_ref, qb_ref, qbk_ref, kbn_ref, ckvn_ref, wukT_ref, wuv_ref,
                     ckv_hbm, kpe_hbm, o_ref, ckv_buf, kpe_buf, sem_c, sem_p, ckvb_ref, s_ref):
    slot = _fetch_sequence_pages(pt_ref, ((ckv_hbm, ckv_buf, sem_c, False), (kpe_hbm, kpe_buf, sem_p, True)))
    t_new = qb_ref.shape[1]
    qb = qb_ref[0]
    qbk = qbk_ref[0]

    def chunk(c, carry):
        ks = pl.ds(pl.multiple_of(c * B_CHUNK, B_CHUNK), B_CHUNK)
        ckv_c = ckv_buf[slot, ks, :].astype(MM_DTYPE)
        ckvb_ref[ks, :] = ckv_c
        kpeT = kpe_buf[slot, :, ks]
        knopeT = lax.dot_general(wukT_ref[...], ckv_c, NT_DIMS, preferred_element_type=F32)
        pe_ss = jnp.sum(kpeT * kpeT, axis=0, keepdims=True)
        kpeT_b = kpeT.astype(MM_DTYPE)
        parts = []
        for h in range(H_B):
            kn = knopeT[h * DN_B:(h + 1) * DN_B]
            rs = lax.rsqrt((jnp.sum(kn * kn, axis=0, keepdims=True) + pe_ss) * (1.0 / DQK_B) + EPS)
            kT = jnp.concatenate([kn.astype(MM_DTYPE), kpeT_b], axis=0)
            parts.append(jnp.dot(qbk[:, h * LANES:h * LANES + DQK_B], kT, preferred_element_type=F32) * rs)
        s_ref[:, ks] = jnp.concatenate(parts, axis=0)
        return carry

    lax.fori_loop(0, PAST_LEN // B_CHUNK, chunk, 0, unroll=2)
    kbn = _pad_rows(kbn_ref[0], LANES)
    s_new = jnp.concatenate([lax.dot_general(qb[:, h * LANES:(h + 1) * LANES], kbn[:, h * LANES:(h + 1) * LANES],
                                             NT_DIMS, preferred_element_type=F32) for h in range(H_B)], axis=0)
    s_new = jnp.where(_new_key_mask(t_new, H_B), s_new, NEG_INF)
    s_past = s_ref[...]
    m = jnp.maximum(jnp.max(s_past, axis=-1, keepdims=True), jnp.max(s_new, axis=-1, keepdims=True))
    p_past = jnp.exp(s_past - m)
    p_new = jnp.exp(s_new - m)
    l = jnp.sum(p_past, axis=-1, keepdims=True) + jnp.sum(p_new, axis=-1, keepdims=True)
    acc = (jnp.dot(p_past.astype(MM_DTYPE), ckvb_ref[...], preferred_element_type=F32)
           + jnp.dot(p_new.astype(MM_DTYPE), _pad_rows(ckvn_ref[0].astype(MM_DTYPE), LANES), preferred_element_type=F32))
    lat = (acc / l).astype(MM_DTYPE)
    outs = [jnp.dot(lat[h * t_new:(h + 1) * t_new], wuv_ref[:, h * DV_B:(h + 1) * DV_B],
                    preferred_element_type=F32) for h in range(H_B)]
    for j in range(H_B // 2):
        o_ref[0, :, j * LANES:(j + 1) * LANES] = jnp.concatenate(outs[2 * j:2 * j + 2], axis=1).astype(o_ref.dtype)


def sample_attention_b(page_table, qb, qbk, kb_new, ckv_new, wukT, wuv, cache_ckv, cache_kpe):
    n, t_new, _ = qb.shape
    seq = lambda a: pl.BlockSpec((1, t_new, a.shape[2]), lambda i, pt: (i, 0, 0))
    full = lambda a: pl.BlockSpec(a.shape, lambda i, pt: (0,) * a.ndim)
    hbm = pl.BlockSpec(memory_space=pl.ANY)
    return pl.pallas_call(
        _sample_b_kernel,
        grid_spec=pltpu.PrefetchScalarGridSpec(
            num_scalar_prefetch=1,
            grid=(n,),
            in_specs=[seq(qb), seq(qbk), seq(kb_new), seq(ckv_new), full(wukT), full(wuv), hbm, hbm],
            out_specs=pl.BlockSpec((1, t_new, H_B * DV_B), lambda i, pt: (i, 0, 0)),
            scratch_shapes=[pltpu.VMEM((2, PAST_LEN, D_C), F32),
                            pltpu.VMEM((2, DR_B, PAST_LEN), F32),
                            pltpu.SemaphoreType.DMA((2,)), pltpu.SemaphoreType.DMA((2,)),
                            pltpu.VMEM((PAST_LEN, D_C), MM_DTYPE), pltpu.VMEM((H_B * t_new, PAST_LEN), F32)]),
        out_shape=jax.ShapeDtypeStruct((n, t_new, H_B * DV_B), MM_DTYPE),
        compiler_params=pltpu.CompilerParams(dimension_semantics=("arbitrary",),
                                             vmem_limit_bytes=SAMPLE_VMEM_LIMIT),
        name="sample_attention_b",
    )(page_table.reshape(-1), qb, qbk, kb_new, ckv_new, wukT, wuv, cache_ckv, jnp.transpose(cache_kpe, (0, 2, 1)))


LOG2E = 1.4426950408889634
GELU_A = 0.7978845608028654
GELU_B = GELU_A * 0.044715


def _sorting_network(n):
    pairs = []
    p = 1
    while p < n:
        k = p
        while k >= 1:
            for j in range(k % p, n - k, 2 * k):
                for i in range(min(k, n - j - k)):
                    if (i + j) // (2 * p) == (i + j + k) // (2 * p):
                        pairs.append((i + j, i + j + k))
            k //= 2
        p *= 2
    return pairs


def _top_values(scores, n_top):
    cols = [scores[r * SUBLANES:(r + 1) * SUBLANES, :] for r in range(scores.shape[0] // SUBLANES)]
    for i, j in _sorting_network(len(cols)):
        cols[i], cols[j] = jnp.maximum(cols[i], cols[j]), jnp.minimum(cols[i], cols[j])
    sub = lax.broadcasted_iota(jnp.int32, cols[0].shape, 0).astype(F32)
    out = []
    for k in range(n_top):
        m = jnp.max(cols[0], axis=0, keepdims=True)
        first = jnp.min(jnp.where(cols[0] == m, sub, float(SUBLANES)), axis=0, keepdims=True)
        popped = sub == first
        live = min(len(cols), n_top + 1 - k)
        for r in range(live):
            cols[r] = jnp.where(popped, cols[r + 1] if r + 1 < len(cols) else NEG_INF, cols[r])
        out.append(m)
    out.append(jnp.max(cols[0], axis=0, keepdims=True))
    return out


def _pop_max(work, row_id):
    m = jnp.max(work, axis=0, keepdims=True)
    first = jnp.min(jnp.where(work == m, row_id, float(work.shape[0])), axis=0, keepdims=True)
    return m, jnp.where(row_id == first, NEG_INF, work)


def _peer_route_kernel(x_ref, o_ref, wout_ref, g_ref, wqT_ref, subk_ref,
                       x1_ref, h2T_ref, sT_ref, thr_ref, top_ref):
    x1 = x_ref[...] + jnp.dot(o_ref[...].astype(MM_DTYPE), wout_ref[...], preferred_element_type=F32)
    x1_ref[...] = x1
    h2 = x1 * lax.rsqrt(jnp.mean(x1 * x1, axis=-1, keepdims=True) + EPS) * g_ref[...]
    h2b = h2.astype(MM_DTYPE)
    h2T_ref[...] = h2.T.astype(h2T_ref.dtype)
    qT = lax.dot_general(wqT_ref[...], h2b, NT_DIMS, preferred_element_type=F32)
    half = D_KEY // 2
    cand_id = lax.broadcasted_iota(jnp.int32, (PEER_CANDIDATES, x_ref.shape[0]), 0).astype(F32)
    for h in range(PEER_HEADS):
        s2 = []
        for p in range(2):
            r = (h * 2 + p) * half
            sT = jnp.dot(subk_ref[h * 2 + p], qT[r:r + half, :].astype(MM_DTYPE), preferred_element_type=F32) * LOG2E
            s2.append(sT)
            for k, row in enumerate(_top_values(sT, PEER_TOPK)):
                top_ref[p, k:k + 1, :] = row
        v0 = top_ref[0]
        v1 = top_ref[1]
        parts = [v0[0:1, :] + v1[0:PEER_TOPK, :]]
        for a in range(1, 8):
            parts.append(v0[a:a + 1, :] + v1[0:8, :])
        parts.append(v0[8:16, :] + v1[0:1, :])
        cand = jnp.concatenate(parts, axis=0)
        work = cand
        for k in range(PEER_TOPK):
            thr, work = _pop_max(work, cand_id)
        nxt = jnp.maximum(jnp.max(work, axis=0, keepdims=True), jnp.maximum(v0[PEER_TOPK:PEER_TOPK + 1, :] + v1[0:1, :],
                                         v0[0:1, :] + v1[PEER_TOPK:PEER_TOPK + 1, :]))
        top = v0[0:1, :] + v1[0:1, :]
        z = jnp.sum(jnp.where(cand >= thr, jnp.exp2(cand - top), 0.0), axis=0, keepdims=True)
        c = top + jnp.log2(z) + 1.0
        sT_ref[h] = s2[0] - c
        sT_ref[PEER_HEADS + h] = s2[1]
        thr_ref[h:h + 1, :] = 0.5 * (thr + nxt) - c


def peer_route(x, o, wout_b, g_ffn, wqT_b, subk_b, *, tile):
    t = x.shape[0]
    row = lambda i: (i, 0)
    col = lambda i: (0, i)
    fixed2 = lambda i: (0, 0)
    return pl.pallas_call(
        _peer_route_kernel,
        grid=(t // tile,),
        in_specs=[pl.BlockSpec((tile, D_MODEL), row),
                  pl.BlockSpec((tile, o.shape[1]), row),
                  pl.BlockSpec(wout_b.shape, fixed2),
                  pl.BlockSpec((1, D_MODEL), fixed2),
                  pl.BlockSpec(wqT_b.shape, fixed2),
                  pl.BlockSpec(subk_b.shape, lambda i: (0, 0, 0))],
        out_specs=[pl.BlockSpec((tile, D_MODEL), row),
                   pl.BlockSpec((D_MODEL, tile), col),
                   pl.BlockSpec((2 * PEER_HEADS, N_KEYS, tile), lambda i: (0, 0, i)),
                   pl.BlockSpec((PEER_HEADS, tile), col)],
        out_shape=[jax.ShapeDtypeStruct((t, D_MODEL), F32),
                   jax.ShapeDtypeStruct((D_MODEL, t), MM_DTYPE),
                   jax.ShapeDtypeStruct((2 * PEER_HEADS, N_KEYS, t), F32),
                   jax.ShapeDtypeStruct((PEER_HEADS, t), F32)],
        scratch_shapes=[pltpu.VMEM((2, PEER_TOPK + 8, tile), F32)],
        compiler_params=pltpu.CompilerParams(dimension_semantics=("parallel",), vmem_limit_bytes=VMEM_LIMIT),
        name="peer_route",
    )(x, o, wout_b, g_ffn.reshape(1, D_MODEL), wqT_b, subk_b)


def _peer_dense_kernel(h2T_ref, u_ref, vT_ref, s0_ref, s1_ref, thr_ref, x1_ref, y_ref, acc_ref):
    ei = pl.program_id(1)
    rows_per_step = s0_ref.shape[1]

    @pl.when(ei == 0)
    def _():
        acc_ref[...] = jnp.zeros_like(acc_ref)

    hT = jnp.dot(u_ref[...], h2T_ref[...], preferred_element_type=F32)
    ws = []
    for ii in range(rows_per_step):
        hid = hT[ii * N_KEYS:(ii + 1) * N_KEYS, :]
        gate = jnp.zeros_like(hid)
        for h in range(PEER_HEADS):
            arg = s0_ref[h, ii:ii + 1, :] + s1_ref[h]
            gate = gate + jnp.where(arg >= thr_ref[h:h + 1, :], jnp.exp2(arg), 0.0)
        gx = gate * hid
        ws.append((gx + gx * jnp.tanh(hid * (GELU_A + GELU_B * (hid * hid)))).astype(MM_DTYPE))
    w = jnp.concatenate(ws, axis=0)
    acc_ref[...] += jnp.dot(vT_ref[...], w, preferred_element_type=F32)

    @pl.when(ei == pl.num_programs(1) - 1)
    def _():
        y_ref[...] = x1_ref[...] + acc_ref[...].T


def peer_dense(h2T, u_b, vT_b, sT, thr, x1, *, tile, etile):
    t = x1.shape[0]
    n_exp = u_b.shape[0]
    rows = etile // N_KEYS
    return pl.pallas_call(
        _peer_dense_kernel,
        grid=(t // tile, n_exp // etile),
        in_specs=[pl.BlockSpec((D_MODEL, tile), lambda ti, ei: (0, ti)),
                  pl.BlockSpec((etile, D_MODEL), lambda ti, ei: (ei, 0)),
                  pl.BlockSpec((D_MODEL, etile), lambda ti, ei: (0, ei)),
                  pl.BlockSpec((PEER_HEADS, rows, tile), lambda ti, ei: (0, ei, ti)),
                  pl.BlockSpec((PEER_HEADS, N_KEYS, tile), lambda ti, ei: (1, 0, ti)),
                  pl.BlockSpec((PEER_HEADS, tile), lambda ti, ei: (0, ti)),
                  pl.BlockSpec((tile, D_MODEL), lambda ti, ei: (ti, 0))],
        out_specs=pl.BlockSpec((tile, D_MODEL), lambda ti, ei: (ti, 0)),
        out_shape=jax.ShapeDtypeStruct((t, D_MODEL), F32),
        scratch_shapes=[pltpu.VMEM((D_MODEL, tile), F32)],
        compiler_params=pltpu.CompilerParams(dimension_semantics=("parallel", "arbitrary"),
                                             vmem_limit_bytes=VMEM_LIMIT),
        name="peer_dense",
    )(h2T, u_b, vT_b, sT, sT, thr, x1)


def peer_prepare(w_out, w_peer_q, peer_sub_keys, peer_u, peer_v):
    return (w_out.astype(MM_DTYPE), w_peer_q.T.astype(MM_DTYPE),
            peer_sub_keys.reshape(2 * PEER_HEADS, N_KEYS, D_KEY // 2).astype(MM_DTYPE),
            peer_u.astype(MM_DTYPE), peer_v.T.astype(MM_DTYPE))


def peer_layer(x2d, o2d, g_ffn, prepared):
    wout_b, wqT_b, subk_b, u_b, vT_b = prepared
    x1, h2T, sT, thr = peer_route(x2d, o2d, wout_b, g_ffn, wqT_b, subk_b, tile=ROUTE_TILE)
    return peer_dense(h2T, u_b, vT_b, sT, thr, x1, tile=PEER_TOKEN_TILE, etile=PEER_EXPERT_TILE)


def kernel(x_prompt, x_sample, cache_a_k, cache_a_v, cache_a_kidx, cache_b_ckv, cache_b_kpe, page_table,
           g_norm_mix, w_in, g_q_a, g_k_a, g_k_idx, g_cq, w_uq, g_ckv, w_uk, w_uv, g_q_b, g_k_b,
           w_out, g_norm_ffn, w_peer_q, peer_sub_keys, peer_u, peer_v):
    n_p, s_len, _ = x_prompt.shape
    n_s, t_new, _ = x_sample.shape
    in_prep = in_proj_prepare(g_norm_mix, w_in, g_q_a, g_k_a, g_k_idx, g_cq, w_uq, g_ckv, w_uk, w_uv, g_q_b, g_k_b)
    wuv_b = in_prep[11]
    peer_prep = peer_prepare(w_out, w_peer_q, peer_sub_keys, peer_u, peer_v)

    rot_p = rotary_tables(jnp.arange(s_len, dtype=F32))
    (_qa, ka, va, kiwi, ckv, kpe, ka_b, va_b, ki_b, qi, qb, kb, vb, _) = in_proj(
        x_prompt.reshape(-1, D_MODEL), rot_p, in_prep, tile=IN_TILE)
    seq_p = lambda a: a.reshape(n_p, s_len, a.shape[-1])
    o_p = prompt_attention(seq_p(_qa), seq_p(qi), seq_p(kiwi), seq_p(qb), seq_p(ki_b), seq_p(ka_b), seq_p(va_b),
                           seq_p(kb), seq_p(vb), q_block=Q_BLOCK, n_classes=CAUSAL_CLASSES)
    yp = peer_layer(x_prompt.reshape(-1, D_MODEL), o_p.reshape(-1, D_MIX), g_norm_ffn, peer_prep)
    prompt_rows = (ka.reshape(n_p, s_len, HKV_A, DH_A), va.reshape(n_p, s_len, HKV_A, DH_A),
                   kiwi[:, :D_IDX].reshape(n_p, s_len, D_IDX), ckv.reshape(n_p, s_len, D_C),
                   kpe[:, PE_LANE:PE_LANE + DR_B].reshape(n_p, s_len, DR_B))

    rot_s = rotary_tables(jnp.tile(PAST_LEN + jnp.arange(t_new, dtype=F32), IN_TILE // t_new))
    (_qa, ka, va, kiwi, ckv, kpe, ka_b, va_b, ki_b, qi, qb, kb, vb, qbk) = in_proj(
        x_sample.reshape(-1, D_MODEL), rot_s, in_prep, tile=IN_TILE)
    seq_s = lambda a: a.reshape(n_s, t_new, a.shape[-1])
    o_a = sample_attention_a(page_table, seq_s(_qa), seq_s(qi), seq_s(kiwi), seq_s(ki_b), seq_s(ka_b), seq_s(va_b),
                             cache_a_kidx, cache_a_k, cache_a_v)
    o_b = sample_attention_b(page_table, seq_s(qb), seq_s(qbk), seq_s(kb), seq_s(ckv), w_uk.T.astype(MM_DTYPE), wuv_b,
                             cache_b_ckv, cache_b_kpe)
    o_s = jnp.concatenate([o_a, o_b], axis=-1)
    ys = peer_layer(x_sample.reshape(-1, D_MODEL), o_s.reshape(-1, D_MIX), g_norm_ffn, peer_prep)
    sample_rows = (ka.reshape(n_s, t_new, HKV_A, DH_A), va.reshape(n_s, t_new, HKV_A, DH_A),
                   kiwi[:, :D_IDX].reshape(n_s, t_new, D_IDX), ckv.reshape(n_s, t_new, D_C),
                   kpe[:, PE_LANE:PE_LANE + DR_B].reshape(n_s, t_new, DR_B))
    return (yp.reshape(x_prompt.shape), ys.reshape(x_sample.shape)) + prompt_rows + sample_rows
```

```python
import functools

import jax
import jax.numpy as jnp
from jax import lax
from jax.experimental import pallas as pl
from jax.experimental.pallas import tpu as pltpu

D_MODEL = 1024
PAST_LEN = 8192
PAGE_SIZE = 128
N_PAGES = PAST_LEN // PAGE_SIZE
ROPE_THETA = 500000.0
EPS = 1e-6
H_A = 8
HKV_A = 2
DH_A = 64
ROT_A = DH_A // 4
H_IDX = 4
D_IDX = 64
TOPK_MAX = 256
H_B = 8
DN_B = 64
DR_B = 32
DV_B = 64
D_CQ = 384
D_C = 256
PEER_HEADS = 8
N_KEYS = 128
D_KEY = 128
PEER_TOPK = 16
IN_WIDTHS = (H_A * DH_A, HKV_A * DH_A, HKV_A * DH_A, H_IDX * D_IDX, D_IDX, H_IDX, D_CQ, D_C, DR_B)
IN_OFFSETS = tuple(sum(IN_WIDTHS[:i]) for i in range(len(IN_WIDTHS)))
D_MIX = H_A * DH_A + H_B * DV_B

MM_DTYPE = jnp.bfloat16
F32 = jnp.float32
NEG_INF = float("-inf")
INT_MIN = -2 ** 31
LANES = 128
SUBLANES = 8
VMEM_LIMIT = 48 * 1024 * 1024
SAMPLE_VMEM_LIMIT = 56 * 1024 * 1024
NT_DIMS = (((1,), (1,)), ((), ()))

C_QA, C_KA, C_VA, C_QI, C_KIWI, C_CQ, C_CKV, C_KPE, W_IN = 0, 512, 640, 768, 1024, 1152, 1536, 1792, 1920
WI_LANE = 80
PE_LANE = DN_B
DQK_B = DN_B + DR_B
HEAD_PAD = LANES - DQK_B
LOG2E = 1.4426950408889634
SCALE_A = DH_A ** -0.5 * LOG2E
SCALE_B = DQK_B ** -0.5 * LOG2E

IN_TILE = 256
ROUTE_TILE = 256
PEER_TOKEN_TILE = 512
PEER_EXPERT_TILE = 2048
Q_BLOCK = 128
CAUSAL_CLASSES = 8
B_CHUNK = 1024
PAGE_LOOP_UNROLL = 8
SAMPLE_SEARCH_BITS = 3
PEER_CANDIDATES = 80


def _seg_sum(sq, ind_ref):
    hi = sq.astype(MM_DTYPE)
    lo = (sq - hi.astype(F32)).astype(MM_DTYPE)
    ind = ind_ref[...]
    return jnp.dot(hi, ind, preferred_element_type=F32) + jnp.dot(lo, ind, preferred_element_type=F32)


def _rot(x, c, sm, sp, half):
    return x * c + pltpu.roll(x, LANES - half, 1) * sm + pltpu.roll(x, half, 1) * sp


def _in_proj_kernel(x_ref, rot_ref, gmix_ref, win_ref, ind_ref, indki_ref, g64_ref, gkiwi_ref, gcq_ref, wuq_ref,
                    gqb_ref, gckv_ref, wuk_ref, wuv_ref, gkb_ref,
                    qa_ref, ka_ref, va_ref, kiwi_ref, ckv_ref, kpe_ref,
                    kab_ref, vab_ref, kib_ref, qi_ref, qb_ref, kb_ref, vb_ref, qbk_ref):
    x = x_ref[...]
    h = x * lax.rsqrt(jnp.mean(x * x, axis=-1, keepdims=True) + EPS) * gmix_ref[...]
    proj = jnp.dot(h.astype(MM_DTYPE), win_ref[...], preferred_element_type=F32)
    c16, sm16, sp16 = rot_ref[0], rot_ref[1], rot_ref[2]
    c32, sm32, sp32 = rot_ref[3], rot_ref[4], rot_ref[5]
    half16, half32 = ROT_A // 2, DR_B // 2

    def norm_rot64(xc, g):
        rs = lax.rsqrt(_seg_sum(xc * xc, ind_ref) * (1.0 / DH_A) + EPS)
        return _rot(xc * rs * g, c16, sm16, sp16, half16)

    for j in range(H_A * DH_A // LANES):
        xc = proj[:, C_QA + j * LANES:C_QA + (j + 1) * LANES]
        qa_ref[:, j * LANES:(j + 1) * LANES] = (norm_rot64(xc, g64_ref[0:1, :]) * SCALE_A).astype(qa_ref.dtype)
    ka = norm_rot64(proj[:, C_KA:C_KA + LANES], g64_ref[1:2, :])
    ka_ref[...] = ka
    kab_ref[...] = ka.astype(kab_ref.dtype)
    va = proj[:, C_VA:C_VA + LANES]
    va_ref[...] = va
    vab_ref[...] = va.astype(vab_ref.dtype)
    for j in range(H_IDX * D_IDX // LANES):
        xc = proj[:, C_QI + j * LANES:C_QI + (j + 1) * LANES]
        qi_ref[:, j * LANES:(j + 1) * LANES] = _rot(xc, c16, sm16, sp16, half16).astype(qi_ref.dtype)
    xc = proj[:, C_KIWI:C_KIWI + LANES]
    lane = lax.broadcasted_iota(jnp.int32, xc.shape, 1)
    rs = lax.rsqrt(_seg_sum(xc * xc, indki_ref) * (1.0 / D_IDX) + EPS)
    kiwi = _rot(xc * jnp.where(lane < D_IDX, rs, 1.0) * gkiwi_ref[...], c16, sm16, sp16, half16)
    kiwi_ref[...] = kiwi
    kib_ref[...] = kiwi[:, :D_IDX].astype(kib_ref.dtype)
    cq = proj[:, C_CQ:C_CQ + D_CQ]
    cqn = cq * lax.rsqrt(jnp.mean(cq * cq, axis=-1, keepdims=True) + EPS) * gcq_ref[...]
    qb = jnp.dot(cqn.astype(MM_DTYPE), wuq_ref[...], preferred_element_type=F32)
    ckv = proj[:, C_CKV:C_CKV + D_C]
    ckvn = ckv * lax.rsqrt(jnp.mean(ckv * ckv, axis=-1, keepdims=True) + EPS) * gckv_ref[...]
    ckv_ref[...] = ckvn
    kpe = _rot(proj[:, C_KPE:C_KPE + LANES], c32, sm32, sp32, half32)
    kpe_ref[...] = kpe
    ckvb = ckvn.astype(MM_DTYPE)
    knope = jnp.dot(ckvb, wuk_ref[...], preferred_element_type=F32)
    vb_ref[...] = jnp.dot(ckvb, wuv_ref[...], preferred_element_type=F32).astype(vb_ref.dtype)
    for hd in range(H_B):
        sl = slice(hd * LANES, (hd + 1) * LANES)
        qh = _rot(qb[:, sl], c32, sm32, sp32, half32)
        qh = qh * lax.rsqrt(jnp.sum(qh * qh, axis=-1, keepdims=True) * (1.0 / DQK_B) + EPS) * (gqb_ref[...] * SCALE_B)
        qb_ref[:, sl] = qh.astype(qb_ref.dtype)
        qbk_ref[:, sl] = (qh * gkb_ref[...]).astype(qbk_ref.dtype)
        kh = knope[:, sl] + kpe
        kh = kh * lax.rsqrt(jnp.sum(kh * kh, axis=-1, keepdims=True) * (1.0 / DQK_B) + EPS) * gkb_ref[...]
        kb_ref[:, sl] = kh.astype(kb_ref.dtype)


def rotary_tables(pos):
    p = pos.shape[0]

    def ang(n_rot):
        inv = ROPE_THETA ** (-jnp.arange(n_rot // 2, dtype=F32) * 2.0 / n_rot)
        return pos[:, None] * inv[None, :]

    a16 = ang(ROT_A)
    cos16, sin16 = jnp.cos(a16), jnp.sin(a16)
    one48 = jnp.ones((p, DH_A - ROT_A), F32)
    zero48 = jnp.zeros((p, DH_A - ROT_A), F32)
    zero8 = jnp.zeros((p, ROT_A // 2), F32)
    c16 = jnp.tile(jnp.concatenate([cos16, cos16, one48], axis=1), (1, 2))
    sm16 = jnp.tile(jnp.concatenate([-sin16, zero8, zero48], axis=1), (1, 2))
    sp16 = jnp.tile(jnp.concatenate([zero8, sin16, zero48], axis=1), (1, 2))
    a32 = ang(DR_B)
    cos32, sin32 = jnp.cos(a32), jnp.sin(a32)
    one64, zero64 = jnp.ones((p, PE_LANE), F32), jnp.zeros((p, PE_LANE), F32)
    one32, zero32 = jnp.ones((p, HEAD_PAD), F32), jnp.zeros((p, HEAD_PAD), F32)
    zero16 = jnp.zeros((p, DR_B // 2), F32)
    c32 = jnp.concatenate([one64, cos32, cos32, one32], axis=1)
    sm32 = jnp.concatenate([zero64, -sin32, zero16, zero32], axis=1)
    sp32 = jnp.concatenate([zero64, zero16, sin32, zero32], axis=1)
    return jnp.stack([c16, sm16, sp16, c32, sm32, sp32], axis=0)


def in_proj_prepare(g_norm_mix, w_in, g_q_a, g_k_a, g_k_idx, g_cq, w_uq, g_ckv, w_uk, w_uv, g_q_b, g_k_b):
    def cols(i, left=0, right=0):
        return jnp.pad(w_in[:, IN_OFFSETS[i]:IN_OFFSETS[i] + IN_WIDTHS[i]], ((0, 0), (left, right)))

    win = jnp.concatenate([cols(0), cols(1), cols(2), cols(3), cols(4, 0, WI_LANE - D_IDX),
                           cols(5, 0, LANES - WI_LANE - H_IDX), cols(6), cols(7),
                           cols(8, PE_LANE, HEAD_PAD)], axis=1).astype(MM_DTYPE)
    lane = jnp.arange(LANES)
    ind = (lane[:, None] // DH_A == lane[None, :] // DH_A).astype(MM_DTYPE)
    indki = ((lane[:, None] < D_IDX) & (lane[None, :] < D_IDX)).astype(MM_DTYPE)
    g64 = jnp.stack([jnp.tile(g_q_a, 2), jnp.tile(g_k_a, 2)], axis=0)
    gkiwi = jnp.concatenate([g_k_idx, jnp.zeros((WI_LANE - D_IDX,), F32),
                             jnp.full((H_IDX,), (H_IDX * D_IDX) ** -0.5, F32),
                             jnp.zeros((LANES - WI_LANE - H_IDX,), F32)]).reshape(1, LANES)
    wuq = jnp.pad(w_uq.reshape(D_CQ, H_B, DQK_B), ((0, 0), (0, 0), (0, HEAD_PAD))).reshape(D_CQ, H_B * LANES)
    wuk = jnp.pad(w_uk.reshape(D_C, H_B, DN_B), ((0, 0), (0, 0), (0, LANES - DN_B))).reshape(D_C, H_B * LANES)
    gqb = jnp.pad(g_q_b, (0, HEAD_PAD)).reshape(1, LANES)
    gkb = jnp.pad(g_k_b, (0, HEAD_PAD)).reshape(1, LANES)
    return (g_norm_mix.reshape(1, D_MODEL), win, ind, indki, g64, gkiwi, g_cq.reshape(1, D_CQ),
            wuq.astype(MM_DTYPE), gqb, g_ckv.reshape(1, D_C), wuk.astype(MM_DTYPE), w_uv.astype(MM_DTYPE), gkb)


def in_proj(x2d, rot, prepared, *, tile):
    t = x2d.shape[0]
    n_rot_blocks = rot.shape[1] // tile
    row = lambda i: (i, 0)
    full = lambda a: pl.BlockSpec(a.shape, lambda i: (0,) * a.ndim)
    widths = [(H_A * DH_A, MM_DTYPE), (LANES, F32), (LANES, F32), (LANES, F32), (D_C, F32), (LANES, F32),
              (LANES, MM_DTYPE), (LANES, MM_DTYPE), (D_IDX, MM_DTYPE), (H_IDX * D_IDX, MM_DTYPE),
              (H_B * LANES, MM_DTYPE), (H_B * LANES, MM_DTYPE), (H_B * DV_B, MM_DTYPE), (H_B * LANES, MM_DTYPE)]
    return pl.pallas_call(
        _in_proj_kernel,
        grid=(t // tile,),
        in_specs=[pl.BlockSpec((tile, D_MODEL), row),
                  pl.BlockSpec((6, tile, LANES), lambda i: (0, i % n_rot_blocks, 0))] + [full(a) for a in prepared],
        out_specs=[pl.BlockSpec((tile, w), row) for w, _ in widths],
        out_shape=[jax.ShapeDtypeStruct((t, w), d) for w, d in widths],
        compiler_params=pltpu.CompilerParams(dimension_semantics=("parallel",), vmem_limit_bytes=VMEM_LIMIT),
        name="in_proj",
    )(x2d, rot, *prepared)


def _sortable_key(x):
    b = pltpu.bitcast(x, jnp.int32)
    return jnp.where(b < 0, b ^ 0x7FFFFFFF, b)


def _count(mask):
    return jnp.sum(jnp.where(mask, 1.0, 0.0), axis=-1, keepdims=True)


def _kth_largest_key(key, k, bits):
    kf = float(k)
    t0 = jnp.where(_count(key >= 0) >= kf, 0, INT_MIN).astype(jnp.int32)
    low_bits = 31

    def step(i, t, width, top):
        unit = jnp.left_shift(jnp.int32(1), top - width * (i + 1))
        for j in range(1, 2 ** width):
            cand = t + j * unit if j == 1 else cand + unit
            best = jnp.where(_count(key >= cand) >= kf, cand, t if j == 1 else best)
        return best

    wide_steps = low_bits // bits
    t = lax.fori_loop(0, wide_steps, lambda i, t: step(i, t, bits, low_bits), t0)
    rest = low_bits - wide_steps * bits
    return lax.fori_loop(0, rest, lambda i, t: step(i, t, 1, rest), t) if rest else t


def _topk_mask(score, k, tri_ref, mask_ref, bits=1):
    key = _sortable_key(score)
    t = _kth_largest_key(key, k, bits)
    ge = key >= t
    mask_ref[...] = jnp.where(ge, 1.0, 0.0)

    @pl.when(jnp.max(_count(ge)) > float(k))
    def _():
        gt = key > t
        eq = key == t
        need = float(k) - _count(gt)
        carry = jnp.zeros_like(need)
        for c in range(score.shape[1] // LANES):
            sl = slice(c * LANES, (c + 1) * LANES)
            eq_c = jnp.where(eq[:, sl], 1.0, 0.0)
            incl = jnp.dot(eq_c.astype(MM_DTYPE), tri_ref[...], preferred_element_type=F32)
            rank = incl - eq_c + carry
            mask_ref[:, sl] = jnp.where(gt[:, sl] | (eq[:, sl] & (rank < need)), 1.0, 0.0)
            carry = carry + jnp.sum(eq_c, axis=-1, keepdims=True)

    return mask_ref[...] > 0.5


def _tie_rank_matrix():
    lane = jnp.arange(LANES)
    return (lane[:, None] <= lane[None, :]).astype(MM_DTYPE)


def _masked_attend(q, k, v, mask):
    s = lax.dot_general(q, k, NT_DIMS, preferred_element_type=F32)
    s = jnp.where(mask, s, NEG_INF)
    p = jnp.exp2(s - jnp.max(s, axis=-1, keepdims=True))
    l = jnp.sum(p, axis=-1, keepdims=True)
    return jnp.dot(p.astype(MM_DTYPE), v, preferred_element_type=F32) / l


def _prompt_attn_kernel(qa_ref, qi_ref, kiwi_ref, qb_ref, ki_ref, ka_ref, va_ref, kb_ref, vb_ref, tri_ref, o_ref,
                        mask_ref, *, k_top, first_block):
    qn = qa_ref.shape[1]
    s_len = ki_ref.shape[1]
    q0 = (first_block + pl.program_id(1)) * qn
    qpos = q0 + lax.broadcasted_iota(jnp.int32, (qn, s_len), 0)
    kpos = lax.broadcasted_iota(jnp.int32, (qn, s_len), 1)
    causal = kpos <= qpos

    ki = ki_ref[0]
    qi = qi_ref[0]
    kiwi = kiwi_ref[0]
    score = jnp.zeros((qn, s_len), F32)
    for h in range(H_IDX):
        s = lax.dot_general(qi[:, h * D_IDX:(h + 1) * D_IDX], ki, NT_DIMS, preferred_element_type=F32)
        score = score + kiwi[:, WI_LANE + h:WI_LANE + h + 1] * jnp.maximum(s, 0.0)
    score = jnp.where(causal, score, NEG_INF)
    sel = _topk_mask(score, k_top, tri_ref, mask_ref) & causal

    qa = qa_ref[0]
    ka = ka_ref[0]
    va = va_ref[0]
    outs = []
    group = H_A // HKV_A
    for h in range(H_A):
        g = h // group
        outs.append(_masked_attend(qa[:, h * DH_A:(h + 1) * DH_A], ka[:, g * DH_A:(g + 1) * DH_A],
                                   va[:, g * DH_A:(g + 1) * DH_A], sel))
    qb = qb_ref[0]
    for h in range(H_B):
        outs.append(_masked_attend(qb[:, h * LANES:(h + 1) * LANES], kb_ref[0, :, h * LANES:(h + 1) * LANES],
                                   vb_ref[0, :, h * DV_B:(h + 1) * DV_B], causal))
    for j in range(len(outs) // 2):
        o_ref[0, :, j * LANES:(j + 1) * LANES] = jnp.concatenate(outs[2 * j:2 * j + 2], axis=1).astype(o_ref.dtype)


def prompt_attention(qa, qi, kiwi, qb, ki, ka, va, kb, vb, *, q_block, n_classes):
    n, s_len, _ = qa.shape
    k_top = min(TOPK_MAX, s_len // 4)
    tri = _tie_rank_matrix()
    class_len = s_len // n_classes
    blocks = class_len // q_block
    outs = []
    for c in range(n_classes):
        first = c * blocks
        s_vis = (c + 1) * class_len
        qspec = lambda a: pl.BlockSpec((1, q_block, a.shape[2]), lambda b, i: (b, first + i, 0))
        kspec = lambda a: pl.BlockSpec((1, s_vis, a.shape[2]), lambda b, i: (b, 0, 0))
        outs.append(pl.pallas_call(
            functools.partial(_prompt_attn_kernel, k_top=k_top, first_block=first),
            grid=(n, blocks),
            in_specs=[qspec(qa), qspec(qi), qspec(kiwi), qspec(qb), kspec(ki), kspec(ka), kspec(va), kspec(kb),
                      kspec(vb), pl.BlockSpec(tri.shape, lambda b, i: (0, 0))],
            out_specs=pl.BlockSpec((1, q_block, D_MIX), lambda b, i: (b, i, 0)),
            out_shape=jax.ShapeDtypeStruct((n, class_len, D_MIX), MM_DTYPE),
            scratch_shapes=[pltpu.VMEM((q_block, s_vis), F32)],
            compiler_params=pltpu.CompilerParams(dimension_semantics=("parallel", "parallel"),
                                                 vmem_limit_bytes=VMEM_LIMIT),
            name="prompt_attention",
        )(qa, qi, kiwi, qb, ki, ka, va, kb, vb, tri))
    return jnp.concatenate(outs, axis=1)


def _page_copy(pt_ref, n, p, slot, cache, buf, sem, keys_last):
    phys = pt_ref[n * N_PAGES + p]
    keys = pl.ds(pl.multiple_of(p * PAGE_SIZE, PAGE_SIZE), PAGE_SIZE)
    lead = (slice(None),) * (len(buf.shape) - 2)
    dst = buf.at[(slot,) + lead + (keys,)] if keys_last else buf.at[slot, keys]
    return pltpu.make_async_copy(cache.at[phys], dst, sem.at[slot])


def _start_pages(pt_ref, n, slot, streams):
    def body(p, carry):
        for cache, buf, sem, keys_last in streams:
            _page_copy(pt_ref, n, p, slot, cache, buf, sem, keys_last).start()
        return carry
    lax.fori_loop(0, N_PAGES, body, 0, unroll=PAGE_LOOP_UNROLL)


def _wait_pages(pt_ref, n, slot, streams):
    def body(p, carry):
        for cache, buf, sem, keys_last in streams:
            _page_copy(pt_ref, n, p, slot, cache, buf, sem, keys_last).wait()
        return carry
    lax.fori_loop(0, N_PAGES, body, 0, unroll=PAGE_LOOP_UNROLL)


def _fetch_sequence_pages(pt_ref, streams):
    n = pl.program_id(0)
    slot = n % 2

    @pl.when(n == 0)
    def _():
        _start_pages(pt_ref, n, slot, streams)

    @pl.when(n + 1 < pl.num_programs(0))
    def _():
        _start_pages(pt_ref, n + 1, 1 - slot, streams)

    _wait_pages(pt_ref, n, slot, streams)
    return slot


def _pad_rows(a, rows):
    return jnp.concatenate([a, jnp.zeros((rows - a.shape[0], a.shape[1]), a.dtype)], axis=0)


def _new_key_mask(t_rows, reps):
    tok = lax.broadcasted_iota(jnp.int32, (reps * t_rows, LANES), 0) % t_rows
    j = lax.broadcasted_iota(jnp.int32, (reps * t_rows, LANES), 1)
    return j <= tok


def _sample_a_kernel(pt_ref, qa_ref, qi_ref, kiwi_ref, kin_ref, kan_ref, van_ref, tri_ref,
                     ckidx_hbm, ck_hbm, cv_hbm, o_ref, kidx_buf, k_buf, v_buf, sem_i, sem_k, sem_v, mask_ref):
    slot = _fetch_sequence_pages(pt_ref, ((ckidx_hbm, kidx_buf, sem_i, True), (ck_hbm, k_buf, sem_k, True),
                                          (cv_hbm, v_buf, sem_v, True)))
    t_new = qa_ref.shape[1]
    qi = qi_ref[0]
    kiwi = kiwi_ref[0]
    qi_rows = jnp.concatenate([qi[:, h * D_IDX:(h + 1) * D_IDX] for h in range(H_IDX)], axis=0)
    sp_all = jnp.dot(qi_rows, kidx_buf[slot].astype(MM_DTYPE), preferred_element_type=F32)
    sn_all = lax.dot_general(qi_rows, _pad_rows(kin_ref[0], LANES), NT_DIMS, preferred_element_type=F32)
    score_p = jnp.zeros((t_new, PAST_LEN), F32)
    score_n = jnp.zeros((t_new, LANES), F32)
    for h in range(H_IDX):
        wh = kiwi[:, WI_LANE + h:WI_LANE + h + 1]
        score_p = score_p + wh * jnp.maximum(sp_all[h * t_new:(h + 1) * t_new], 0.0)
        score_n = score_n + wh * jnp.maximum(sn_all[h * t_new:(h + 1) * t_new], 0.0)
    vis_n = _new_key_mask(t_new, 1)
    score = jnp.concatenate([score_p, jnp.where(vis_n, score_n, NEG_INF)], axis=1)
    k_top = min(TOPK_MAX, (PAST_LEN + t_new) // 4)
    visible = jnp.concatenate([jnp.full((t_new, PAST_LEN), True), vis_n], axis=1)
    sel = _topk_mask(score, k_top, tri_ref, mask_ref, bits=SAMPLE_SEARCH_BITS) & visible

    group = H_A // HKV_A
    sel_g = jnp.concatenate([sel] * group, axis=0)
    qa = qa_ref[0]
    outs = []
    for g in range(HKV_A):
        gs = slice(g * DH_A, (g + 1) * DH_A)
        q_g = jnp.concatenate([qa[:, (g * group + j) * DH_A:(g * group + j + 1) * DH_A] for j in range(group)], axis=0)
        kn_g = _pad_rows(kan_ref[0][:, gs], LANES)
        vn_g = _pad_rows(van_ref[0][:, gs], LANES)
        s = jnp.concatenate([jnp.dot(q_g, k_buf[slot, g].astype(MM_DTYPE), preferred_element_type=F32),
                             lax.dot_general(q_g, kn_g, NT_DIMS, preferred_element_type=F32)], axis=1)
        s = jnp.where(sel_g, s, NEG_INF)
        p = jnp.exp2(s - jnp.max(s, axis=-1, keepdims=True))
        l = jnp.sum(p, axis=-1, keepdims=True)
        pb = p.astype(MM_DTYPE)
        o_g = (lax.dot_general(pb[:, :PAST_LEN], v_buf[slot, g].astype(MM_DTYPE), NT_DIMS, preferred_element_type=F32)
               + jnp.dot(pb[:, PAST_LEN:], vn_g, preferred_element_type=F32)) / l
        outs += [o_g[j * t_new:(j + 1) * t_new] for j in range(group)]
    for j in range(len(outs) // 2):
        o_ref[0, :, j * LANES:(j + 1) * LANES] = jnp.concatenate(outs[2 * j:2 * j + 2], axis=1).astype(o_ref.dtype)


def sample_attention_a(page_table, qa, qi, kiwi, ki_new, ka_new, va_new, cache_kidx, cache_k, cache_v):
    n, t_new, _ = qa.shape
    tri = _tie_rank_matrix()
    seq = lambda a: pl.BlockSpec((1, t_new, a.shape[2]), lambda i, pt: (i, 0, 0))
    hbm = pl.BlockSpec(memory_space=pl.ANY)
    return pl.pallas_call(
        _sample_a_kernel,
        grid_spec=pltpu.PrefetchScalarGridSpec(
            num_scalar_prefetch=1,
            grid=(n,),
            in_specs=[seq(qa), seq(qi), seq(kiwi), seq(ki_new), seq(ka_new), seq(va_new),
                      pl.BlockSpec(tri.shape, lambda i, pt: (0, 0)), hbm, hbm, hbm],
            out_specs=pl.BlockSpec((1, t_new, H_A * DH_A), lambda i, pt: (i, 0, 0)),
            scratch_shapes=[pltpu.VMEM((2, D_IDX, PAST_LEN), F32),
                            pltpu.VMEM((2, HKV_A, DH_A, PAST_LEN), F32),
                            pltpu.VMEM((2, HKV_A, DH_A, PAST_LEN), F32),
                            pltpu.SemaphoreType.DMA((2,)), pltpu.SemaphoreType.DMA((2,)),
                            pltpu.SemaphoreType.DMA((2,)), pltpu.VMEM((t_new, PAST_LEN + LANES), F32)]),
        out_shape=jax.ShapeDtypeStruct((n, t_new, H_A * DH_A), MM_DTYPE),
        compiler_params=pltpu.CompilerParams(dimension_semantics=("arbitrary",),
                                             vmem_limit_bytes=SAMPLE_VMEM_LIMIT),
        name="sample_attention_a",
    )(page_table.reshape(-1), qa, qi, kiwi, ki_new, ka_new, va_new, tri,
      jnp.transpose(cache_kidx, (0, 2, 1)), jnp.transpose(cache_k, (0, 2, 3, 1)), jnp.transpose(cache_v, (0, 2, 3, 1)))


def _sample_b_kernel(pt_ref, qb_ref, qbk_ref, kbn_ref, ckvn_ref, wukT_ref, wuv_ref,
                     ckv_hbm, kpe_hbm, o_ref, ckv_buf, kpe_buf, sem_c, sem_p, ckvb_ref, s_ref):
    slot = _fetch_sequence_pages(pt_ref, ((ckv_hbm, ckv_buf, sem_c, False), (kpe_hbm, kpe_buf, sem_p, True)))
    t_new = qb_ref.shape[1]
    qb = qb_ref[0]
    qbk = qbk_ref[0]

    def chunk(c, carry):
        ks = pl.ds(pl.multiple_of(c * B_CHUNK, B_CHUNK), B_CHUNK)
        ckv_c = ckv_buf[slot, ks, :].astype(MM_DTYPE)
        ckvb_ref[ks, :] = ckv_c
        kpeT = kpe_buf[slot, :, ks]
        knopeT = lax.dot_general(wukT_ref[...], ckv_c, NT_DIMS, preferred_element_type=F32)
        pe_ss = jnp.sum(kpeT * kpeT, axis=0, keepdims=True)
        kpeT_b = kpeT.astype(MM_DTYPE)
        parts = []
        for h in range(H_B):
            kn = knopeT[h * DN_B:(h + 1) * DN_B]
            rs = lax.rsqrt((jnp.sum(kn * kn, axis=0, keepdims=True) + pe_ss) * (1.0 / DQK_B) + EPS)
            kT = jnp.concatenate([kn.astype(MM_DTYPE), kpeT_b], axis=0)
            parts.append(jnp.dot(qbk[:, h * LANES:h * LANES + DQK_B], kT, preferred_element_type=F32) * rs)
        s_ref[:, ks] = jnp.concatenate(parts, axis=0)
        return carry

    lax.fori_loop(0, PAST_LEN // B_CHUNK, chunk, 0, unroll=4)
    kbn = _pad_rows(kbn_ref[0], LANES)
    s_new = jnp.concatenate([lax.dot_general(qb[:, h * LANES:(h + 1) * LANES], kbn[:, h * LANES:(h + 1) * LANES],
                                             NT_DIMS, preferred_element_type=F32) for h in range(H_B)], axis=0)
    s_new = jnp.where(_new_key_mask(t_new, H_B), s_new, NEG_INF)
    s_past = s_ref[...]
    m = jnp.maximum(jnp.max(s_past, axis=-1, keepdims=True), jnp.max(s_new, axis=-1, keepdims=True))
    p_past = jnp.exp2(s_past - m)
    p_new = jnp.exp2(s_new - m)
    l = jnp.sum(p_past, axis=-1, keepdims=True) + jnp.sum(p_new, axis=-1, keepdims=True)
    acc = (jnp.dot(p_past.astype(MM_DTYPE), ckvb_ref[...], preferred_element_type=F32)
           + jnp.dot(p_new.astype(MM_DTYPE), _pad_rows(ckvn_ref[0].astype(MM_DTYPE), LANES), preferred_element_type=F32))
    lat = (acc / l).astype(MM_DTYPE)
    outs = [jnp.dot(lat[h * t_new:(h + 1) * t_new], wuv_ref[:, h * DV_B:(h + 1) * DV_B],
                    preferred_element_type=F32) for h in range(H_B)]
    for j in range(H_B // 2):
        o_ref[0, :, j * LANES:(j + 1) * LANES] = jnp.concatenate(outs[2 * j:2 * j + 2], axis=1).astype(o_ref.dtype)


def sample_attention_b(page_table, qb, qbk, kb_new, ckv_new, wukT, wuv, cache_ckv, cache_kpe):
    n, t_new, _ = qb.shape
    seq = lambda a: pl.BlockSpec((1, t_new, a.shape[2]), lambda i, pt: (i, 0, 0))
    full = lambda a: pl.BlockSpec(a.shape, lambda i, pt: (0,) * a.ndim)
    hbm = pl.BlockSpec(memory_space=pl.ANY)
    return pl.pallas_call(
        _sample_b_kernel,
        grid_spec=pltpu.PrefetchScalarGridSpec(
            num_scalar_prefetch=1,
            grid=(n,),
            in_specs=[seq(qb), seq(qbk), seq(kb_new), seq(ckv_new), full(wukT), full(wuv), hbm, hbm],
            out_specs=pl.BlockSpec((1, t_new, H_B * DV_B), lambda i, pt: (i, 0, 0)),
            scratch_shapes=[pltpu.VMEM((2, PAST_LEN, D_C), F32),
                            pltpu.VMEM((2, DR_B, PAST_LEN), F32),
                            pltpu.SemaphoreType.DMA((2,)), pltpu.SemaphoreType.DMA((2,)),
                            pltpu.VMEM((PAST_LEN, D_C), MM_DTYPE), pltpu.VMEM((H_B * t_new, PAST_LEN), F32)]),
        out_shape=jax.ShapeDtypeStruct((n, t_new, H_B * DV_B), MM_DTYPE),
        compiler_params=pltpu.CompilerParams(dimension_semantics=("arbitrary",),
                                             vmem_limit_bytes=SAMPLE_VMEM_LIMIT),
        name="sample_attention_b",
    )(page_table.reshape(-1), qb, qbk, kb_new, ckv_new, wukT, wuv, cache_ckv, jnp.transpose(cache_kpe, (0, 2, 1)))


GELU_A = 0.7978845608028654
GELU_B = GELU_A * 0.044715


def _sorting_network(n):
    pairs = []
    p = 1
    while p < n:
        k = p
        while k >= 1:
            for j in range(k % p, n - k, 2 * k):
                for i in range(min(k, n - j - k)):
                    if (i + j) // (2 * p) == (i + j + k) // (2 * p):
                        pairs.append((i + j, i + j + k))
            k //= 2
        p *= 2
    return pairs


def _top_values(scores, n_top):
    cols = [scores[r * SUBLANES:(r + 1) * SUBLANES, :] for r in range(scores.shape[0] // SUBLANES)]
    n_real = len(cols)
    cols += [None] * (pl.next_power_of_2(n_real) - n_real)
    for i, j in _sorting_network(len(cols)):
        if cols[j] is None:
            continue
        if cols[i] is None:
            cols[i], cols[j] = cols[j], None
        else:
            cols[i], cols[j] = jnp.maximum(cols[i], cols[j]), jnp.minimum(cols[i], cols[j])
    cols = cols[:n_real]
    sub = lax.broadcasted_iota(jnp.int32, cols[0].shape, 0).astype(F32)
    out = []
    for k in range(n_top):
        m = jnp.max(cols[0], axis=0, keepdims=True)
        first = jnp.min(jnp.where(cols[0] == m, sub, float(SUBLANES)), axis=0, keepdims=True)
        popped = sub == first
        live = min(len(cols), n_top + 1 - k)
        for r in range(live):
            cols[r] = jnp.where(popped, cols[r + 1] if r + 1 < len(cols) else NEG_INF, cols[r])
        out.append(m)
    out.append(jnp.max(cols[0], axis=0, keepdims=True))
    return out


def _peer_route_kernel(x_ref, o_ref, wout_ref, g_ref, wqT_ref, subk_ref,
                       x1_ref, h2T_ref, sT_ref, thr_ref, top_ref):
    x1 = x_ref[...] + jnp.dot(o_ref[...].astype(MM_DTYPE), wout_ref[...], preferred_element_type=F32)
    x1_ref[...] = x1
    h2 = x1 * lax.rsqrt(jnp.mean(x1 * x1, axis=-1, keepdims=True) + EPS) * g_ref[...]
    h2b = h2.astype(MM_DTYPE)
    h2T_ref[...] = h2.T.astype(h2T_ref.dtype)
    qT = lax.dot_general(wqT_ref[...], h2b, NT_DIMS, preferred_element_type=F32)
    half = D_KEY // 2
    for h in range(PEER_HEADS):
        s2 = []
        for p in range(2):
            r = (h * 2 + p) * half
            sT = jnp.dot(subk_ref[h * 2 + p], qT[r:r + half, :].astype(MM_DTYPE), preferred_element_type=F32) * LOG2E
            s2.append(sT)
            for k, row in enumerate(_top_values(sT, PEER_TOPK)):
                top_ref[p, k:k + 1, :] = row
        v0 = top_ref[0]
        v1 = top_ref[1]
        parts = [v0[0:1, :] + v1[0:PEER_TOPK, :]]
        for a in range(1, 8):
            parts.append(v0[a:a + 1, :] + v1[0:8, :])
        parts.append(v0[8:16, :] + v1[0:1, :])
        cand = jnp.concatenate(parts, axis=0)
        best = _top_values(cand, PEER_TOPK)
        thr = best[PEER_TOPK - 1]
        nxt = jnp.maximum(best[PEER_TOPK], jnp.maximum(v0[PEER_TOPK:PEER_TOPK + 1, :] + v1[0:1, :],
                                         v0[0:1, :] + v1[PEER_TOPK:PEER_TOPK + 1, :]))
        top = v0[0:1, :] + v1[0:1, :]
        z = jnp.sum(jnp.where(cand >= thr, jnp.exp2(cand - top), 0.0), axis=0, keepdims=True)
        c = top + jnp.log2(z) + 1.0
        sT_ref[h] = s2[0] - c
        sT_ref[PEER_HEADS + h] = s2[1]
        thr_ref[h:h + 1, :] = 0.5 * (thr + nxt) - c


def peer_route(x, o, wout_b, g_ffn, wqT_b, subk_b, *, tile):
    t = x.shape[0]
    row = lambda i: (i, 0)
    col = lambda i: (0, i)
    fixed2 = lambda i: (0, 0)
    return pl.pallas_call(
        _peer_route_kernel,
        grid=(t // tile,),
        in_specs=[pl.BlockSpec((tile, D_MODEL), row),
                  pl.BlockSpec((tile, o.shape[1]), row),
                  pl.BlockSpec(wout_b.shape, fixed2),
                  pl.BlockSpec((1, D_MODEL), fixed2),
                  pl.BlockSpec(wqT_b.shape, fixed2),
                  pl.BlockSpec(subk_b.shape, lambda i: (0, 0, 0))],
        out_specs=[pl.BlockSpec((tile, D_MODEL), row),
                   pl.BlockSpec((D_MODEL, tile), col),
                   pl.BlockSpec((2 * PEER_HEADS, N_KEYS, tile), lambda i: (0, 0, i)),
                   pl.BlockSpec((PEER_HEADS, tile), col)],
        out_shape=[jax.ShapeDtypeStruct((t, D_MODEL), F32),
                   jax.ShapeDtypeStruct((D_MODEL, t), MM_DTYPE),
                   jax.ShapeDtypeStruct((2 * PEER_HEADS, N_KEYS, t), F32),
                   jax.ShapeDtypeStruct((PEER_HEADS, t), F32)],
        scratch_shapes=[pltpu.VMEM((2, PEER_TOPK + 8, tile), F32)],
        compiler_params=pltpu.CompilerParams(dimension_semantics=("parallel",), vmem_limit_bytes=VMEM_LIMIT),
        name="peer_route",
    )(x, o, wout_b, g_ffn.reshape(1, D_MODEL), wqT_b, subk_b)


def _peer_dense_kernel(h2T_ref, u_ref, vT_ref, s0_ref, s1_ref, thr_ref, x1_ref, y_ref, acc_ref):
    ei = pl.program_id(1)
    rows_per_step = s0_ref.shape[1]

    @pl.when(ei == 0)
    def _():
        acc_ref[...] = jnp.zeros_like(acc_ref)

    hT = jnp.dot(u_ref[...], h2T_ref[...], preferred_element_type=F32)
    ws = []
    for ii in range(rows_per_step):
        hid = hT[ii * N_KEYS:(ii + 1) * N_KEYS, :]
        gate = jnp.zeros_like(hid)
        for h in range(PEER_HEADS):
            arg = s0_ref[h, ii:ii + 1, :] + s1_ref[h]
            gate = gate + jnp.where(arg >= thr_ref[h:h + 1, :], jnp.exp2(arg), 0.0)
        gx = gate * hid
        ws.append((gx + gx * jnp.tanh(hid * (GELU_A + GELU_B * (hid * hid)))).astype(MM_DTYPE))
    w = jnp.concatenate(ws, axis=0)
    acc_ref[...] += jnp.dot(vT_ref[...], w, preferred_element_type=F32)

    @pl.when(ei == pl.num_programs(1) - 1)
    def _():
        y_ref[...] = x1_ref[...] + acc_ref[...].T


def peer_dense(h2T, u_b, vT_b, sT, thr, x1, *, tile, etile):
    t = x1.shape[0]
    n_exp = u_b.shape[0]
    rows = etile // N_KEYS
    return pl.pallas_call(
        _peer_dense_kernel,
        grid=(t // tile, n_exp // etile),
        in_specs=[pl.BlockSpec((D_MODEL, tile), lambda ti, ei: (0, ti)),
                  pl.BlockSpec((etile, D_MODEL), lambda ti, ei: (ei, 0)),
                  pl.BlockSpec((D_MODEL, etile), lambda ti, ei: (0, ei)),
                  pl.BlockSpec((PEER_HEADS, rows, tile), lambda ti, ei: (0, ei, ti)),
                  pl.BlockSpec((PEER_HEADS, N_KEYS, tile), lambda ti, ei: (1, 0, ti)),
                  pl.BlockSpec((PEER_HEADS, tile), lambda ti, ei: (0, ti)),
                  pl.BlockSpec((tile, D_MODEL), lambda ti, ei: (ti, 0))],
        out_specs=pl.BlockSpec((tile, D_MODEL), lambda ti, ei: (ti, 0)),
        out_shape=jax.ShapeDtypeStruct((t, D_MODEL), F32),
        scratch_shapes=[pltpu.VMEM((D_MODEL, tile), F32)],
        compiler_params=pltpu.CompilerParams(dimension_semantics=("parallel", "arbitrary"),
                                             vmem_limit_bytes=VMEM_LIMIT),
        name="peer_dense",
    )(h2T, u_b, vT_b, sT, sT, thr, x1)


def peer_prepare(w_out, w_peer_q, peer_sub_keys, peer_u, peer_v):
    return (w_out.astype(MM_DTYPE), w_peer_q.T.astype(MM_DTYPE),
            peer_sub_keys.reshape(2 * PEER_HEADS, N_KEYS, D_KEY // 2).astype(MM_DTYPE),
            peer_u.astype(MM_DTYPE), peer_v.T.astype(MM_DTYPE))


def peer_layer(x2d, o2d, g_ffn, prepared):
    wout_b, wqT_b, subk_b, u_b, vT_b = prepared
    x1, h2T, sT, thr = peer_route(x2d, o2d, wout_b, g_ffn, wqT_b, subk_b, tile=ROUTE_TILE)
    return peer_dense(h2T, u_b, vT_b, sT, thr, x1, tile=PEER_TOKEN_TILE, etile=PEER_EXPERT_TILE)


def kernel(x_prompt, x_sample, cache_a_k, cache_a_v, cache_a_kidx, cache_b_ckv, cache_b_kpe, page_table,
           g_norm_mix, w_in, g_q_a, g_k_a, g_k_idx, g_cq, w_uq, g_ckv, w_uk, w_uv, g_q_b, g_k_b,
           w_out, g_norm_ffn, w_peer_q, peer_sub_keys, peer_u, peer_v):
    n_p, s_len, _ = x_prompt.shape
    n_s, t_new, _ = x_sample.shape
    in_prep = in_proj_prepare(g_norm_mix, w_in, g_q_a, g_k_a, g_k_idx, g_cq, w_uq, g_ckv, w_uk, w_uv, g_q_b, g_k_b)
    wuv_b = in_prep[11]
    peer_prep = peer_prepare(w_out, w_peer_q, peer_sub_keys, peer_u, peer_v)

    rot_p = rotary_tables(jnp.arange(s_len, dtype=F32))
    (_qa, ka, va, kiwi, ckv, kpe, ka_b, va_b, ki_b, qi, qb, kb, vb, _) = in_proj(
        x_prompt.reshape(-1, D_MODEL), rot_p, in_prep, tile=IN_TILE)
    seq_p = lambda a: a.reshape(n_p, s_len, a.shape[-1])
    o_p = prompt_attention(seq_p(_qa), seq_p(qi), seq_p(kiwi), seq_p(qb), seq_p(ki_b), seq_p(ka_b), seq_p(va_b),
                           seq_p(kb), seq_p(vb), q_block=Q_BLOCK, n_classes=CAUSAL_CLASSES)
    yp = peer_layer(x_prompt.reshape(-1, D_MODEL), o_p.reshape(-1, D_MIX), g_norm_ffn, peer_prep)
    prompt_rows = (ka.reshape(n_p, s_len, HKV_A, DH_A), va.reshape(n_p, s_len, HKV_A, DH_A),
                   kiwi[:, :D_IDX].reshape(n_p, s_len, D_IDX), ckv.reshape(n_p, s_len, D_C),
                   kpe[:, PE_LANE:PE_LANE + DR_B].reshape(n_p, s_len, DR_B))

    rot_s = rotary_tables(jnp.tile(PAST_LEN + jnp.arange(t_new, dtype=F32), IN_TILE // t_new))
    (_qa, ka, va, kiwi, ckv, kpe, ka_b, va_b, ki_b, qi, qb, kb, vb, qbk) = in_proj(
        x_sample.reshape(-1, D_MODEL), rot_s, in_prep, tile=IN_TILE)
    seq_s = lambda a: a.reshape(n_s, t_new, a.shape[-1])
    o_a = sample_attention_a(page_table, seq_s(_qa), seq_s(qi), seq_s(kiwi), seq_s(ki_b), seq_s(ka_b), seq_s(va_b),
                             cache_a_kidx, cache_a_k, cache_a_v)
    o_b = sample_attention_b(page_table, seq_s(qb), seq_s(qbk), seq_s(kb), seq_s(ckv), w_uk.T.astype(MM_DTYPE), wuv_b,
                             cache_b_ckv, cache_b_kpe)
    o_s = jnp.concatenate([o_a, o_b], axis=-1)
    ys = peer_layer(x_sample.reshape(-1, D_MODEL), o_s.reshape(-1, D_MIX), g_norm_ffn, peer_prep)
    sample_rows = (ka.reshape(n_s, t_new, HKV_A, DH_A), va.reshape(n_s, t_new, HKV_A, DH_A),
                   kiwi[:, :D_IDX].reshape(n_s, t_new, D_IDX), ckv.reshape(n_s, t_new, D_C),
                   kpe[:, PE_LANE:PE_LANE + DR_B].reshape(n_s, t_new, DR_B))
    return (yp.reshape(x_prompt.shape), ys.reshape(x_sample.shape)) + prompt_rows + sample_rows
```

```python
import functools

import jax
import jax.numpy as jnp
from jax import lax
from jax.experimental import pallas as pl
from jax.experimental.pallas import tpu as pltpu

D_MODEL = 1024
PAST_LEN = 8192
PAGE_SIZE = 128
N_PAGES = PAST_LEN // PAGE_SIZE
ROPE_THETA = 500000.0
EPS = 1e-6
H_A = 8
HKV_A = 2
DH_A = 64
ROT_A = DH_A // 4
H_IDX = 4
D_IDX = 64
TOPK_MAX = 256
H_B = 8
DN_B = 64
DR_B = 32
DV_B = 64
D_CQ = 384
D_C = 256
PEER_HEADS = 8
N_KEYS = 128
D_KEY = 128
PEER_TOPK = 16
IN_WIDTHS = (H_A * DH_A, HKV_A * DH_A, HKV_A * DH_A, H_IDX * D_IDX, D_IDX, H_IDX, D_CQ, D_C, DR_B)
IN_OFFSETS = tuple(sum(IN_WIDTHS[:i]) for i in range(len(IN_WIDTHS)))
D_MIX = H_A * DH_A + H_B * DV_B

MM_DTYPE = jnp.bfloat16
F32 = jnp.float32
NEG_INF = float("-inf")
INT_MIN = -2 ** 31
KEY_NEG_INF = INT_MIN + 0x007FFFFF
LANES = 128
SUBLANES = 8
VMEM_LIMIT = 48 * 1024 * 1024
SAMPLE_VMEM_LIMIT = 56 * 1024 * 1024
NT_DIMS = (((1,), (1,)), ((), ()))

C_QA, C_KA, C_VA, C_QI, C_KIWI, C_CQ, C_CKV, C_KPE, W_IN = 0, 512, 640, 768, 1024, 1152, 1536, 1792, 1920
WI_LANE = 80
PE_LANE = DN_B
DQK_B = DN_B + DR_B
HEAD_PAD = LANES - DQK_B
LOG2E = 1.4426950408889634
SCALE_A = DH_A ** -0.5 * LOG2E
SCALE_B = DQK_B ** -0.5 * LOG2E

IN_TILE = 256
ROUTE_TILE = 256
PEER_TOKEN_TILE = 512
PEER_EXPERT_TILE = 2048
Q_BLOCK = 256
CAUSAL_CLASSES = 8
B_CHUNK = 1024
PAGE_LOOP_UNROLL = 8
SAMPLE_SEARCH_BITS = 3
PEER_CANDIDATES = 80


def _seg_sum(sq, ind_ref):
    hi = sq.astype(MM_DTYPE)
    lo = (sq - hi.astype(F32)).astype(MM_DTYPE)
    ind = ind_ref[...]
    return jnp.dot(hi, ind, preferred_element_type=F32) + jnp.dot(lo, ind, preferred_element_type=F32)


def _rot(x, c, sm, sp, half):
    return x * c + pltpu.roll(x, LANES - half, 1) * sm + pltpu.roll(x, half, 1) * sp


def _in_proj_kernel(x_ref, rot_ref, gmix_ref, win_ref, ind_ref, indki_ref, g64_ref, gkiwi_ref, gcq_ref, wuq_ref,
                    gqb_ref, gckv_ref, wuk_ref, wuv_ref, gkb_ref,
                    qa_ref, ka_ref, va_ref, kiwi_ref, ckv_ref, kpe_ref,
                    kab_ref, vab_ref, kib_ref, qi_ref, qb_ref, kb_ref, vb_ref, qbk_ref):
    x = x_ref[...]
    h = x * lax.rsqrt(jnp.mean(x * x, axis=-1, keepdims=True) + EPS) * gmix_ref[...]
    proj = jnp.dot(h.astype(MM_DTYPE), win_ref[...], preferred_element_type=F32)
    c16, sm16, sp16 = rot_ref[0], rot_ref[1], rot_ref[2]
    c32, sm32, sp32 = rot_ref[3], rot_ref[4], rot_ref[5]
    half16, half32 = ROT_A // 2, DR_B // 2

    def norm_rot64(xc, g):
        rs = lax.rsqrt(_seg_sum(xc * xc, ind_ref) * (1.0 / DH_A) + EPS)
        return _rot(xc * rs * g, c16, sm16, sp16, half16)

    for j in range(H_A * DH_A // LANES):
        xc = proj[:, C_QA + j * LANES:C_QA + (j + 1) * LANES]
        qa_ref[:, j * LANES:(j + 1) * LANES] = (norm_rot64(xc, g64_ref[0:1, :]) * SCALE_A).astype(qa_ref.dtype)
    ka = norm_rot64(proj[:, C_KA:C_KA + LANES], g64_ref[1:2, :])
    ka_ref[...] = ka
    kab_ref[...] = ka.astype(kab_ref.dtype)
    va = proj[:, C_VA:C_VA + LANES]
    va_ref[...] = va
    vab_ref[...] = va.astype(vab_ref.dtype)
    for j in range(H_IDX * D_IDX // LANES):
        xc = proj[:, C_QI + j * LANES:C_QI + (j + 1) * LANES]
        qi_ref[:, j * LANES:(j + 1) * LANES] = _rot(xc, c16, sm16, sp16, half16).astype(qi_ref.dtype)
    xc = proj[:, C_KIWI:C_KIWI + LANES]
    lane = lax.broadcasted_iota(jnp.int32, xc.shape, 1)
    rs = lax.rsqrt(_seg_sum(xc * xc, indki_ref) * (1.0 / D_IDX) + EPS)
    kiwi = _rot(xc * jnp.where(lane < D_IDX, rs, 1.0) * gkiwi_ref[...], c16, sm16, sp16, half16)
    kiwi_ref[...] = kiwi
    kib_ref[...] = kiwi[:, :D_IDX].astype(kib_ref.dtype)
    cq = proj[:, C_CQ:C_CQ + D_CQ]
    cqn = cq * lax.rsqrt(jnp.mean(cq * cq, axis=-1, keepdims=True) + EPS) * gcq_ref[...]
    qb = jnp.dot(cqn.astype(MM_DTYPE), wuq_ref[...], preferred_element_type=F32)
    ckv = proj[:, C_CKV:C_CKV + D_C]
    ckvn = ckv * lax.rsqrt(jnp.mean(ckv * ckv, axis=-1, keepdims=True) + EPS) * gckv_ref[...]
    ckv_ref[...] = ckvn
    kpe = _rot(proj[:, C_KPE:C_KPE + LANES], c32, sm32, sp32, half32)
    kpe_ref[...] = kpe
    ckvb = ckvn.astype(MM_DTYPE)
    knope = jnp.dot(ckvb, wuk_ref[...], preferred_element_type=F32)
    vb_ref[...] = jnp.dot(ckvb, wuv_ref[...], preferred_element_type=F32).astype(vb_ref.dtype)
    for hd in range(H_B):
        sl = slice(hd * LANES, (hd + 1) * LANES)
        qh = _rot(qb[:, sl], c32, sm32, sp32, half32)
        qh = qh * lax.rsqrt(jnp.sum(qh * qh, axis=-1, keepdims=True) * (1.0 / DQK_B) + EPS) * (gqb_ref[...] * SCALE_B)
        qb_ref[:, sl] = qh.astype(qb_ref.dtype)
        qbk_ref[:, sl] = (qh * gkb_ref[...]).astype(qbk_ref.dtype)
        kh = knope[:, sl] + kpe
        kh = kh * lax.rsqrt(jnp.sum(kh * kh, axis=-1, keepdims=True) * (1.0 / DQK_B) + EPS) * gkb_ref[...]
        kb_ref[:, sl] = kh.astype(kb_ref.dtype)


def rotary_tables(pos):
    p = pos.shape[0]

    def ang(n_rot):
        inv = ROPE_THETA ** (-jnp.arange(n_rot // 2, dtype=F32) * 2.0 / n_rot)
        return pos[:, None] * inv[None, :]

    a16 = ang(ROT_A)
    cos16, sin16 = jnp.cos(a16), jnp.sin(a16)
    one48 = jnp.ones((p, DH_A - ROT_A), F32)
    zero48 = jnp.zeros((p, DH_A - ROT_A), F32)
    zero8 = jnp.zeros((p, ROT_A // 2), F32)
    c16 = jnp.tile(jnp.concatenate([cos16, cos16, one48], axis=1), (1, 2))
    sm16 = jnp.tile(jnp.concatenate([-sin16, zero8, zero48], axis=1), (1, 2))
    sp16 = jnp.tile(jnp.concatenate([zero8, sin16, zero48], axis=1), (1, 2))
    a32 = ang(DR_B)
    cos32, sin32 = jnp.cos(a32), jnp.sin(a32)
    one64, zero64 = jnp.ones((p, PE_LANE), F32), jnp.zeros((p, PE_LANE), F32)
    one32, zero32 = jnp.ones((p, HEAD_PAD), F32), jnp.zeros((p, HEAD_PAD), F32)
    zero16 = jnp.zeros((p, DR_B // 2), F32)
    c32 = jnp.concatenate([one64, cos32, cos32, one32], axis=1)
    sm32 = jnp.concatenate([zero64, -sin32, zero16, zero32], axis=1)
    sp32 = jnp.concatenate([zero64, zero16, sin32, zero32], axis=1)
    return jnp.stack([c16, sm16, sp16, c32, sm32, sp32], axis=0)


def in_proj_prepare(g_norm_mix, w_in, g_q_a, g_k_a, g_k_idx, g_cq, w_uq, g_ckv, w_uk, w_uv, g_q_b, g_k_b):
    def cols(i, left=0, right=0):
        return jnp.pad(w_in[:, IN_OFFSETS[i]:IN_OFFSETS[i] + IN_WIDTHS[i]], ((0, 0), (left, right)))

    win = jnp.concatenate([cols(0), cols(1), cols(2), cols(3), cols(4, 0, WI_LANE - D_IDX),
                           cols(5, 0, LANES - WI_LANE - H_IDX), cols(6), cols(7),
                           cols(8, PE_LANE, HEAD_PAD)], axis=1).astype(MM_DTYPE)
    lane = jnp.arange(LANES)
    ind = (lane[:, None] // DH_A == lane[None, :] // DH_A).astype(MM_DTYPE)
    indki = ((lane[:, None] < D_IDX) & (lane[None, :] < D_IDX)).astype(MM_DTYPE)
    g64 = jnp.stack([jnp.tile(g_q_a, 2), jnp.tile(g_k_a, 2)], axis=0)
    gkiwi = jnp.concatenate([g_k_idx, jnp.zeros((WI_LANE - D_IDX,), F32),
                             jnp.full((H_IDX,), (H_IDX * D_IDX) ** -0.5, F32),
                             jnp.zeros((LANES - WI_LANE - H_IDX,), F32)]).reshape(1, LANES)
    wuq = jnp.pad(w_uq.reshape(D_CQ, H_B, DQK_B), ((0, 0), (0, 0), (0, HEAD_PAD))).reshape(D_CQ, H_B * LANES)
    wuk = jnp.pad(w_uk.reshape(D_C, H_B, DN_B), ((0, 0), (0, 0), (0, LANES - DN_B))).reshape(D_C, H_B * LANES)
    gqb = jnp.pad(g_q_b, (0, HEAD_PAD)).reshape(1, LANES)
    gkb = jnp.pad(g_k_b, (0, HEAD_PAD)).reshape(1, LANES)
    return (g_norm_mix.reshape(1, D_MODEL), win, ind, indki, g64, gkiwi, g_cq.reshape(1, D_CQ),
            wuq.astype(MM_DTYPE), gqb, g_ckv.reshape(1, D_C), wuk.astype(MM_DTYPE), w_uv.astype(MM_DTYPE), gkb)


def in_proj(x2d, rot, prepared, *, tile):
    t = x2d.shape[0]
    n_rot_blocks = rot.shape[1] // tile
    row = lambda i: (i, 0)
    full = lambda a: pl.BlockSpec(a.shape, lambda i: (0,) * a.ndim)
    widths = [(H_A * DH_A, MM_DTYPE), (LANES, F32), (LANES, F32), (LANES, F32), (D_C, F32), (LANES, F32),
              (LANES, MM_DTYPE), (LANES, MM_DTYPE), (D_IDX, MM_DTYPE), (H_IDX * D_IDX, MM_DTYPE),
              (H_B * LANES, MM_DTYPE), (H_B * LANES, MM_DTYPE), (H_B * DV_B, MM_DTYPE), (H_B * LANES, MM_DTYPE)]
    return pl.pallas_call(
        _in_proj_kernel,
        grid=(t // tile,),
        in_specs=[pl.BlockSpec((tile, D_MODEL), row),
                  pl.BlockSpec((6, tile, LANES), lambda i: (0, i % n_rot_blocks, 0))] + [full(a) for a in prepared],
        out_specs=[pl.BlockSpec((tile, w), row) for w, _ in widths],
        out_shape=[jax.ShapeDtypeStruct((t, w), d) for w, d in widths],
        compiler_params=pltpu.CompilerParams(dimension_semantics=("parallel",), vmem_limit_bytes=VMEM_LIMIT),
        name="in_proj",
    )(x2d, rot, *prepared)


def _key_to_float(key):
    return pltpu.bitcast(jnp.where(key < 0, key ^ 0x7FFFFFFF, key), F32)


def _count(mask):
    return jnp.sum(jnp.where(mask, 1.0, 0.0), axis=-1, keepdims=True)


def _kth_largest(score, k, bits):
    kf = float(k)
    t0 = jnp.where(_count(score >= 0.0) >= kf, 0, INT_MIN).astype(jnp.int32)
    low_bits = 31

    def step(i, t, width, top):
        unit = jnp.left_shift(jnp.int32(1), top - width * (i + 1))
        for j in range(1, 2 ** width):
            cand = t + j * unit if j == 1 else cand + unit
            best = jnp.where(_count(score >= _key_to_float(cand)) >= kf, cand, t if j == 1 else best)
        return best

    wide_steps = low_bits // bits
    t = lax.fori_loop(0, wide_steps, lambda i, t: step(i, t, bits, low_bits), t0)
    rest = low_bits - wide_steps * bits
    if rest:
        t = lax.fori_loop(0, rest, lambda i, t: step(i, t, 1, rest), t)
    return jnp.where(t < KEY_NEG_INF, NEG_INF, _key_to_float(t))


def _topk_mask(score, k, tri_ref, mask_ref, bits=1):
    thr = _kth_largest(score, k, bits)
    ge = score >= thr
    mask_ref[...] = jnp.where(ge, 1.0, 0.0)

    @pl.when(jnp.max(_count(ge)) > float(k))
    def _():
        gt = score > thr
        eq = score == thr
        need = float(k) - _count(gt)
        carry = jnp.zeros_like(need)
        for c in range(score.shape[1] // LANES):
            sl = slice(c * LANES, (c + 1) * LANES)
            eq_c = jnp.where(eq[:, sl], 1.0, 0.0)
            incl = jnp.dot(eq_c.astype(MM_DTYPE), tri_ref[...], preferred_element_type=F32)
            rank = incl - eq_c + carry
            mask_ref[:, sl] = jnp.where(gt[:, sl] | (eq[:, sl] & (rank < need)), 1.0, 0.0)
            carry = carry + jnp.sum(eq_c, axis=-1, keepdims=True)

    return mask_ref[...] > 0.5


def _tie_rank_matrix():
    lane = jnp.arange(LANES)
    return (lane[:, None] <= lane[None, :]).astype(MM_DTYPE)


def _masked_attend(q, k, v, mask):
    s = lax.dot_general(q, k, NT_DIMS, preferred_element_type=F32)
    s = jnp.where(mask, s, NEG_INF)
    p = jnp.exp2(s - jnp.max(s, axis=-1, keepdims=True))
    l = jnp.sum(p, axis=-1, keepdims=True)
    return jnp.dot(p.astype(MM_DTYPE), v, preferred_element_type=F32) / l


def _prompt_attn_kernel(qa_ref, qi_ref, kiwi_ref, qb_ref, ki_ref, ka_ref, va_ref, kb_ref, vb_ref, tri_ref, o_ref,
                        mask_ref, *, k_top, first_block):
    qn = qa_ref.shape[1]
    s_len = ki_ref.shape[1]
    q0 = (first_block + pl.program_id(1)) * qn
    qpos = q0 + lax.broadcasted_iota(jnp.int32, (qn, s_len), 0)
    kpos = lax.broadcasted_iota(jnp.int32, (qn, s_len), 1)
    causal = kpos <= qpos

    ki = ki_ref[0]
    qi = qi_ref[0]
    kiwi = kiwi_ref[0]
    score = jnp.zeros((qn, s_len), F32)
    for h in range(H_IDX):
        s = lax.dot_general(qi[:, h * D_IDX:(h + 1) * D_IDX], ki, NT_DIMS, preferred_element_type=F32)
        score = score + kiwi[:, WI_LANE + h:WI_LANE + h + 1] * jnp.maximum(s, 0.0)
    score = jnp.where(causal, score, NEG_INF)
    sel = _topk_mask(score, k_top, tri_ref, mask_ref) & causal

    qa = qa_ref[0]
    ka = ka_ref[0]
    va = va_ref[0]
    outs = []
    group = H_A // HKV_A
    for h in range(H_A):
        g = h // group
        outs.append(_masked_attend(qa[:, h * DH_A:(h + 1) * DH_A], ka[:, g * DH_A:(g + 1) * DH_A],
                                   va[:, g * DH_A:(g + 1) * DH_A], sel))
    qb = qb_ref[0]
    for h in range(H_B):
        outs.append(_masked_attend(qb[:, h * LANES:(h + 1) * LANES], kb_ref[0, :, h * LANES:(h + 1) * LANES],
                                   vb_ref[0, :, h * DV_B:(h + 1) * DV_B], causal))
    for j in range(len(outs) // 2):
        o_ref[0, :, j * LANES:(j + 1) * LANES] = jnp.concatenate(outs[2 * j:2 * j + 2], axis=1).astype(o_ref.dtype)


def prompt_attention(qa, qi, kiwi, qb, ki, ka, va, kb, vb, *, q_block, n_classes):
    n, s_len, _ = qa.shape
    k_top = min(TOPK_MAX, s_len // 4)
    tri = _tie_rank_matrix()
    class_len = s_len // n_classes
    blocks = class_len // q_block
    outs = []
    for c in range(n_classes):
        first = c * blocks
        s_vis = (c + 1) * class_len
        qspec = lambda a: pl.BlockSpec((1, q_block, a.shape[2]), lambda b, i: (b, first + i, 0))
        kspec = lambda a: pl.BlockSpec((1, s_vis, a.shape[2]), lambda b, i: (b, 0, 0))
        outs.append(pl.pallas_call(
            functools.partial(_prompt_attn_kernel, k_top=k_top, first_block=first),
            grid=(n, blocks),
            in_specs=[qspec(qa), qspec(qi), qspec(kiwi), qspec(qb), kspec(ki), kspec(ka), kspec(va), kspec(kb),
                      kspec(vb), pl.BlockSpec(tri.shape, lambda b, i: (0, 0))],
            out_specs=pl.BlockSpec((1, q_block, D_MIX), lambda b, i: (b, i, 0)),
            out_shape=jax.ShapeDtypeStruct((n, class_len, D_MIX), MM_DTYPE),
            scratch_shapes=[pltpu.VMEM((q_block, s_vis), F32)],
            compiler_params=pltpu.CompilerParams(dimension_semantics=("parallel", "parallel"),
                                                 vmem_limit_bytes=VMEM_LIMIT),
            name="prompt_attention",
        )(qa, qi, kiwi, qb, ki, ka, va, kb, vb, tri))
    return jnp.concatenate(outs, axis=1)


def _page_copy(pt_ref, n, p, slot, cache, buf, sem, keys_last):
    phys = pt_ref[n * N_PAGES + p]
    keys = pl.ds(pl.multiple_of(p * PAGE_SIZE, PAGE_SIZE), PAGE_SIZE)
    lead = (slice(None),) * (len(buf.shape) - 2)
    dst = buf.at[(slot,) + lead + (keys,)] if keys_last else buf.at[slot, keys]
    return pltpu.make_async_copy(cache.at[phys], dst, sem.at[slot])


def _start_pages(pt_ref, n, slot, streams):
    def body(p, carry):
        for cache, buf, sem, keys_last in streams:
            _page_copy(pt_ref, n, p, slot, cache, buf, sem, keys_last).start()
        return carry
    lax.fori_loop(0, N_PAGES, body, 0, unroll=PAGE_LOOP_UNROLL)


def _wait_pages(pt_ref, n, slot, streams):
    def body(p, carry):
        for cache, buf, sem, keys_last in streams:
            _page_copy(pt_ref, n, p, slot, cache, buf, sem, keys_last).wait()
        return carry
    lax.fori_loop(0, N_PAGES, body, 0, unroll=PAGE_LOOP_UNROLL)


def _fetch_sequence_pages(pt_ref, streams):
    n = pl.program_id(0)
    slot = n % 2

    @pl.when(n == 0)
    def _():
        _start_pages(pt_ref, n, slot, streams)

    @pl.when(n + 1 < pl.num_programs(0))
    def _():
        _start_pages(pt_ref, n + 1, 1 - slot, streams)

    _wait_pages(pt_ref, n, slot, streams)
    return slot


def _pad_rows(a, rows):
    return jnp.concatenate([a, jnp.zeros((rows - a.shape[0], a.shape[1]), a.dtype)], axis=0)


def _new_key_mask(t_rows, reps):
    tok = lax.broadcasted_iota(jnp.int32, (reps * t_rows, LANES), 0) % t_rows
    j = lax.broadcasted_iota(jnp.int32, (reps * t_rows, LANES), 1)
    return j <= tok


def _sample_a_kernel(pt_ref, qa_ref, qi_ref, kiwi_ref, kin_ref, kan_ref, van_ref, tri_ref,
                     ckidx_hbm, ck_hbm, cv_hbm, o_ref, kidx_buf, k_buf, v_buf, sem_i, sem_k, sem_v, mask_ref):
    slot = _fetch_sequence_pages(pt_ref, ((ckidx_hbm, kidx_buf, sem_i, True), (ck_hbm, k_buf, sem_k, True),
                                          (cv_hbm, v_buf, sem_v, True)))
    t_new = qa_ref.shape[1]
    qi = qi_ref[0]
    kiwi = kiwi_ref[0]
    qi_rows = jnp.concatenate([qi[:, h * D_IDX:(h + 1) * D_IDX] for h in range(H_IDX)], axis=0)
    sp_all = jnp.dot(qi_rows, kidx_buf[slot].astype(MM_DTYPE), preferred_element_type=F32)
    sn_all = lax.dot_general(qi_rows, _pad_rows(kin_ref[0], LANES), NT_DIMS, preferred_element_type=F32)
    score_p = jnp.zeros((t_new, PAST_LEN), F32)
    score_n = jnp.zeros((t_new, LANES), F32)
    for h in range(H_IDX):
        wh = kiwi[:, WI_LANE + h:WI_LANE + h + 1]
        score_p = score_p + wh * jnp.maximum(sp_all[h * t_new:(h + 1) * t_new], 0.0)
        score_n = score_n + wh * jnp.maximum(sn_all[h * t_new:(h + 1) * t_new], 0.0)
    vis_n = _new_key_mask(t_new, 1)
    score = jnp.concatenate([score_p, jnp.where(vis_n, score_n, NEG_INF)], axis=1)
    k_top = min(TOPK_MAX, (PAST_LEN + t_new) // 4)
    visible = jnp.concatenate([jnp.full((t_new, PAST_LEN), True), vis_n], axis=1)
    sel = _topk_mask(score, k_top, tri_ref, mask_ref, bits=SAMPLE_SEARCH_BITS) & visible

    group = H_A // HKV_A
    sel_g = jnp.concatenate([sel] * group, axis=0)
    qa = qa_ref[0]
    outs = []
    for g in range(HKV_A):
        gs = slice(g * DH_A, (g + 1) * DH_A)
        q_g = jnp.concatenate([qa[:, (g * group + j) * DH_A:(g * group + j + 1) * DH_A] for j in range(group)], axis=0)
        kn_g = _pad_rows(kan_ref[0][:, gs], LANES)
        vn_g = _pad_rows(van_ref[0][:, gs], LANES)
        s = jnp.concatenate([jnp.dot(q_g, k_buf[slot, g].astype(MM_DTYPE), preferred_element_type=F32),
                             lax.dot_general(q_g, kn_g, NT_DIMS, preferred_element_type=F32)], axis=1)
        s = jnp.where(sel_g, s, NEG_INF)
        p = jnp.exp2(s - jnp.max(s, axis=-1, keepdims=True))
        l = jnp.sum(p, axis=-1, keepdims=True)
        pb = p.astype(MM_DTYPE)
        o_g = (lax.dot_general(pb[:, :PAST_LEN], v_buf[slot, g].astype(MM_DTYPE), NT_DIMS, preferred_element_type=F32)
               + jnp.dot(pb[:, PAST_LEN:], vn_g, preferred_element_type=F32)) / l
        outs += [o_g[j * t_new:(j + 1) * t_new] for j in range(group)]
    for j in range(len(outs) // 2):
        o_ref[0, :, j * LANES:(j + 1) * LANES] = jnp.concatenate(outs[2 * j:2 * j + 2], axis=1).astype(o_ref.dtype)


def sample_attention_a(page_table, qa, qi, kiwi, ki_new, ka_new, va_new, cache_kidx, cache_k, cache_v):
    n, t_new, _ = qa.shape
    tri = _tie_rank_matrix()
    seq = lambda a: pl.BlockSpec((1, t_new, a.shape[2]), lambda i, pt: (i, 0, 0))
    hbm = pl.BlockSpec(memory_space=pl.ANY)
    return pl.pallas_call(
        _sample_a_kernel,
        grid_spec=pltpu.PrefetchScalarGridSpec(
            num_scalar_prefetch=1,
            grid=(n,),
            in_specs=[seq(qa), seq(qi), seq(kiwi), seq(ki_new), seq(ka_new), seq(va_new),
                      pl.BlockSpec(tri.shape, lambda i, pt: (0, 0)), hbm, hbm, hbm],
            out_specs=pl.BlockSpec((1, t_new, H_A * DH_A), lambda i, pt: (i, 0, 0)),
            scratch_shapes=[pltpu.VMEM((2, D_IDX, PAST_LEN), F32),
                            pltpu.VMEM((2, HKV_A, DH_A, PAST_LEN), F32),
                            pltpu.VMEM((2, HKV_A, DH_A, PAST_LEN), F32),
                            pltpu.SemaphoreType.DMA((2,)), pltpu.SemaphoreType.DMA((2,)),
                            pltpu.SemaphoreType.DMA((2,)), pltpu.VMEM((t_new, PAST_LEN + LANES), F32)]),
        out_shape=jax.ShapeDtypeStruct((n, t_new, H_A * DH_A), MM_DTYPE),
        compiler_params=pltpu.CompilerParams(dimension_semantics=("arbitrary",),
                                             vmem_limit_bytes=SAMPLE_VMEM_LIMIT),
        name="sample_attention_a",
    )(page_table.reshape(-1), qa, qi, kiwi, ki_new, ka_new, va_new, tri,
      jnp.transpose(cache_kidx, (0, 2, 1)), jnp.transpose(cache_k, (0, 2, 3, 1)), jnp.transpose(cache_v, (0, 2, 3, 1)))


def _sample_b_kernel(pt_ref, qb_ref, qbk_ref, kbn_ref, ckvn_ref, wukT_ref, wuv_ref,
                     ckv_hbm, kpe_hbm, o_ref, ckv_buf, kpe_buf, sem_c, sem_p, ckvb_ref, s_ref):
    slot = _fetch_sequence_pages(pt_ref, ((ckv_hbm, ckv_buf, sem_c, False), (kpe_hbm, kpe_buf, sem_p, True)))
    t_new = qb_ref.shape[1]
    qb = qb_ref[0]
    qbk = qbk_ref[0]

    def chunk(c, carry):
        ks = pl.ds(pl.multiple_of(c * B_CHUNK, B_CHUNK), B_CHUNK)
        ckv_c = ckv_buf[slot, ks, :].astype(MM_DTYPE)
        ckvb_ref[ks, :] = ckv_c
        kpeT = kpe_buf[slot, :, ks]
        knopeT = lax.dot_general(wukT_ref[...], ckv_c, NT_DIMS, preferred_element_type=F32)
        pe_ss = jnp.sum(kpeT * kpeT, axis=0, keepdims=True)
        kpeT_b = kpeT.astype(MM_DTYPE)
        parts = []
        for h in range(H_B):
            kn = knopeT[h * DN_B:(h + 1) * DN_B]
            rs = lax.rsqrt((jnp.sum(kn * kn, axis=0, keepdims=True) + pe_ss) * (1.0 / DQK_B) + EPS)
            kT = jnp.concatenate([kn.astype(MM_DTYPE), kpeT_b], axis=0)
            parts.append(jnp.dot(qbk[:, h * LANES:h * LANES + DQK_B], kT, preferred_element_type=F32) * rs)
        s_ref[:, ks] = jnp.concatenate(parts, axis=0)
        return carry

    lax.fori_loop(0, PAST_LEN // B_CHUNK, chunk, 0, unroll=4)
    kbn = _pad_rows(kbn_ref[0], LANES)
    s_new = jnp.concatenate([lax.dot_general(qb[:, h * LANES:(h + 1) * LANES], kbn[:, h * LANES:(h + 1) * LANES],
                                             NT_DIMS, preferred_element_type=F32) for h in range(H_B)], axis=0)
    s_new = jnp.where(_new_key_mask(t_new, H_B), s_new, NEG_INF)
    s_past = s_ref[...]
    m = jnp.maximum(jnp.max(s_past, axis=-1, keepdims=True), jnp.max(s_new, axis=-1, keepdims=True))
    p_past = jnp.exp2(s_past - m)
    p_new = jnp.exp2(s_new - m)
    l = jnp.sum(p_past, axis=-1, keepdims=True) + jnp.sum(p_new, axis=-1, keepdims=True)
    acc = (jnp.dot(p_past.astype(MM_DTYPE), ckvb_ref[...], preferred_element_type=F32)
           + jnp.dot(p_new.astype(MM_DTYPE), _pad_rows(ckvn_ref[0].astype(MM_DTYPE), LANES), preferred_element_type=F32))
    lat = (acc / l).astype(MM_DTYPE)
    outs = [jnp.dot(lat[h * t_new:(h + 1) * t_new], wuv_ref[:, h * DV_B:(h + 1) * DV_B],
                    preferred_element_type=F32) for h in range(H_B)]
    for j in range(H_B // 2):
        o_ref[0, :, j * LANES:(j + 1) * LANES] = jnp.concatenate(outs[2 * j:2 * j + 2], axis=1).astype(o_ref.dtype)


def sample_attention_b(page_table, qb, qbk, kb_new, ckv_new, wukT, wuv, cache_ckv, cache_kpe):
    n, t_new, _ = qb.shape
    seq = lambda a: pl.BlockSpec((1, t_new, a.shape[2]), lambda i, pt: (i, 0, 0))
    full = lambda a: pl.BlockSpec(a.shape, lambda i, pt: (0,) * a.ndim)
    hbm = pl.BlockSpec(memory_space=pl.ANY)
    return pl.pallas_call(
        _sample_b_kernel,
        grid_spec=pltpu.PrefetchScalarGridSpec(
            num_scalar_prefetch=1,
            grid=(n,),
            in_specs=[seq(qb), seq(qbk), seq(kb_new), seq(ckv_new), full(wukT), full(wuv), hbm, hbm],
            out_specs=pl.BlockSpec((1, t_new, H_B * DV_B), lambda i, pt: (i, 0, 0)),
            scratch_shapes=[pltpu.VMEM((2, PAST_LEN, D_C), F32),
                            pltpu.VMEM((2, DR_B, PAST_LEN), F32),
                            pltpu.SemaphoreType.DMA((2,)), pltpu.SemaphoreType.DMA((2,)),
                            pltpu.VMEM((PAST_LEN, D_C), MM_DTYPE), pltpu.VMEM((H_B * t_new, PAST_LEN), F32)]),
        out_shape=jax.ShapeDtypeStruct((n, t_new, H_B * DV_B), MM_DTYPE),
        compiler_params=pltpu.CompilerParams(dimension_semantics=("arbitrary",),
                                             vmem_limit_bytes=SAMPLE_VMEM_LIMIT),
        name="sample_attention_b",
    )(page_table.reshape(-1), qb, qbk, kb_new, ckv_new, wukT, wuv, cache_ckv, jnp.transpose(cache_kpe, (0, 2, 1)))


GELU_A = 0.7978845608028654
GELU_B = GELU_A * 0.044715


def _sorting_network(n):
    pairs = []
    p = 1
    while p < n:
        k = p
        while k >= 1:
            for j in range(k % p, n - k, 2 * k):
                for i in range(min(k, n - j - k)):
                    if (i + j) // (2 * p) == (i + j + k) // (2 * p):
                        pairs.append((i + j, i + j + k))
            k //= 2
        p *= 2
    return pairs


def _top_values(scores, n_top):
    cols = [scores[r * SUBLANES:(r + 1) * SUBLANES, :] for r in range(scores.shape[0] // SUBLANES)]
    n_real = len(cols)
    cols += [None] * (pl.next_power_of_2(n_real) - n_real)
    for i, j in _sorting_network(len(cols)):
        if cols[j] is None:
            continue
        if cols[i] is None:
            cols[i], cols[j] = cols[j], None
        else:
            cols[i], cols[j] = jnp.maximum(cols[i], cols[j]), jnp.minimum(cols[i], cols[j])
    cols = cols[:n_real]
    sub = lax.broadcasted_iota(jnp.int32, cols[0].shape, 0).astype(F32)
    out = []
    for k in range(n_top):
        m = jnp.max(cols[0], axis=0, keepdims=True)
        first = jnp.min(jnp.where(cols[0] == m, sub, float(SUBLANES)), axis=0, keepdims=True)
        popped = sub == first
        live = min(len(cols), n_top + 1 - k)
        for r in range(live):
            cols[r] = jnp.where(popped, cols[r + 1] if r + 1 < len(cols) else NEG_INF, cols[r])
        out.append(m)
    out.append(jnp.max(cols[0], axis=0, keepdims=True))
    return out


def _peer_route_kernel(x_ref, o_ref, wout_ref, g_ref, wqT_ref, subk_ref,
                       x1_ref, h2T_ref, sT_ref, thr_ref, top_ref):
    x1 = x_ref[...] + jnp.dot(o_ref[...].astype(MM_DTYPE), wout_ref[...], preferred_element_type=F32)
    x1_ref[...] = x1
    h2 = x1 * lax.rsqrt(jnp.mean(x1 * x1, axis=-1, keepdims=True) + EPS) * g_ref[...]
    h2b = h2.astype(MM_DTYPE)
    h2T_ref[...] = h2.T.astype(h2T_ref.dtype)
    qT = lax.dot_general(wqT_ref[...], h2b, NT_DIMS, preferred_element_type=F32)
    half = D_KEY // 2
    for h in range(PEER_HEADS):
        s2 = []
        for p in range(2):
            r = (h * 2 + p) * half
            sT = jnp.dot(subk_ref[h * 2 + p], qT[r:r + half, :].astype(MM_DTYPE), preferred_element_type=F32) * LOG2E
            s2.append(sT)
            for k, row in enumerate(_top_values(sT, PEER_TOPK)):
                top_ref[p, k:k + 1, :] = row
        v0 = top_ref[0]
        v1 = top_ref[1]
        parts = [v0[0:1, :] + v1[0:PEER_TOPK, :]]
        for a in range(1, 8):
            parts.append(v0[a:a + 1, :] + v1[0:8, :])
        parts.append(v0[8:16, :] + v1[0:1, :])
        cand = jnp.concatenate(parts, axis=0)
        best = _top_values(cand, PEER_TOPK)
        thr = best[PEER_TOPK - 1]
        nxt = jnp.maximum(best[PEER_TOPK], jnp.maximum(v0[PEER_TOPK:PEER_TOPK + 1, :] + v1[0:1, :],
                                         v0[0:1, :] + v1[PEER_TOPK:PEER_TOPK + 1, :]))
        top = v0[0:1, :] + v1[0:1, :]
        z = jnp.sum(jnp.where(cand >= thr, jnp.exp2(cand - top), 0.0), axis=0, keepdims=True)
        c = top + jnp.log2(z) + 1.0
        sT_ref[h] = s2[0] - c
        sT_ref[PEER_HEADS + h] = s2[1]
        thr_ref[h:h + 1, :] = 0.5 * (thr + nxt) - c


def peer_route(x, o, wout_b, g_ffn, wqT_b, subk_b, *, tile):
    t = x.shape[0]
    row = lambda i: (i, 0)
    col = lambda i: (0, i)
    fixed2 = lambda i: (0, 0)
    return pl.pallas_call(
        _peer_route_kernel,
        grid=(t // tile,),
        in_specs=[pl.BlockSpec((tile, D_MODEL), row),
                  pl.BlockSpec((tile, o.shape[1]), row),
                  pl.BlockSpec(wout_b.shape, fixed2),
                  pl.BlockSpec((1, D_MODEL), fixed2),
                  pl.BlockSpec(wqT_b.shape, fixed2),
                  pl.BlockSpec(subk_b.shape, lambda i: (0, 0, 0))],
        out_specs=[pl.BlockSpec((tile, D_MODEL), row),
                   pl.BlockSpec((D_MODEL, tile), col),
                   pl.BlockSpec((2 * PEER_HEADS, N_KEYS, tile), lambda i: (0, 0, i)),
                   pl.BlockSpec((PEER_HEADS, tile), col)],
        out_shape=[jax.ShapeDtypeStruct((t, D_MODEL), F32),
                   jax.ShapeDtypeStruct((D_MODEL, t), MM_DTYPE),
                   jax.ShapeDtypeStruct((2 * PEER_HEADS, N_KEYS, t), F32),
                   jax.ShapeDtypeStruct((PEER_HEADS, t), F32)],
        scratch_shapes=[pltpu.VMEM((2, PEER_TOPK + 8, tile), F32)],
        compiler_params=pltpu.CompilerParams(dimension_semantics=("parallel",), vmem_limit_bytes=VMEM_LIMIT),
        name="peer_route",
    )(x, o, wout_b, g_ffn.reshape(1, D_MODEL), wqT_b, subk_b)


def _peer_dense_kernel(h2T_ref, u_ref, vT_ref, s0_ref, s1_ref, thr_ref, x1_ref, y_ref, acc_ref):
    ei = pl.program_id(1)
    rows_per_step = s0_ref.shape[1]

    @pl.when(ei == 0)
    def _():
        acc_ref[...] = jnp.zeros_like(acc_ref)

    hT = jnp.dot(u_ref[...], h2T_ref[...], preferred_element_type=F32)
    ws = []
    for ii in range(rows_per_step):
        hid = hT[ii * N_KEYS:(ii + 1) * N_KEYS, :]
        gate = jnp.zeros_like(hid)
        for h in range(PEER_HEADS):
            arg = s0_ref[h, ii:ii + 1, :] + s1_ref[h]
            gate = gate + jnp.where(arg >= thr_ref[h:h + 1, :], jnp.exp2(arg), 0.0)
        gx = gate * hid
        ws.append((gx + gx * jnp.tanh(hid * (GELU_A + GELU_B * (hid * hid)))).astype(MM_DTYPE))
    w = jnp.concatenate(ws, axis=0)
    acc_ref[...] += jnp.dot(vT_ref[...], w, preferred_element_type=F32)

    @pl.when(ei == pl.num_programs(1) - 1)
    def _():
        y_ref[...] = x1_ref[...] + acc_ref[...].T


def peer_dense(h2T, u_b, vT_b, sT, thr, x1, *, tile, etile):
    t = x1.shape[0]
    n_exp = u_b.shape[0]
    rows = etile // N_KEYS
    return pl.pallas_call(
        _peer_dense_kernel,
        grid=(t // tile, n_exp // etile),
        in_specs=[pl.BlockSpec((D_MODEL, tile), lambda ti, ei: (0, ti)),
                  pl.BlockSpec((etile, D_MODEL), lambda ti, ei: (ei, 0)),
                  pl.BlockSpec((D_MODEL, etile), lambda ti, ei: (0, ei)),
                  pl.BlockSpec((PEER_HEADS, rows, tile), lambda ti, ei: (0, ei, ti)),
                  pl.BlockSpec((PEER_HEADS, N_KEYS, tile), lambda ti, ei: (1, 0, ti)),
                  pl.BlockSpec((PEER_HEADS, tile), lambda ti, ei: (0, ti)),
                  pl.BlockSpec((tile, D_MODEL), lambda ti, ei: (ti, 0))],
        out_specs=pl.BlockSpec((tile, D_MODEL), lambda ti, ei: (ti, 0)),
        out_shape=jax.ShapeDtypeStruct((t, D_MODEL), F32),
        scratch_shapes=[pltpu.VMEM((D_MODEL, tile), F32)],
        compiler_params=pltpu.CompilerParams(dimension_semantics=("parallel", "arbitrary"),
                                             vmem_limit_bytes=VMEM_LIMIT),
        name="peer_dense",
    )(h2T, u_b, vT_b, sT, sT, thr, x1)


def peer_prepare(w_out, w_peer_q, peer_sub_keys, peer_u, peer_v):
    return (w_out.astype(MM_DTYPE), w_peer_q.T.astype(MM_DTYPE),
            peer_sub_keys.reshape(2 * PEER_HEADS, N_KEYS, D_KEY // 2).astype(MM_DTYPE),
            peer_u.astype(MM_DTYPE), peer_v.T.astype(MM_DTYPE))


def peer_layer(x2d, o2d, g_ffn, prepared):
    wout_b, wqT_b, subk_b, u_b, vT_b = prepared
    x1, h2T, sT, thr = peer_route(x2d, o2d, wout_b, g_ffn, wqT_b, subk_b, tile=ROUTE_TILE)
    return peer_dense(h2T, u_b, vT_b, sT, thr, x1, tile=PEER_TOKEN_TILE, etile=PEER_EXPERT_TILE)


def kernel(x_prompt, x_sample, cache_a_k, cache_a_v, cache_a_kidx, cache_b_ckv, cache_b_kpe, page_table,
           g_norm_mix, w_in, g_q_a, g_k_a, g_k_idx, g_cq, w_uq, g_ckv, w_uk, w_uv, g_q_b, g_k_b,
           w_out, g_norm_ffn, w_peer_q, peer_sub_keys, peer_u, peer_v):
    n_p, s_len, _ = x_prompt.shape
    n_s, t_new, _ = x_sample.shape
    in_prep = in_proj_prepare(g_norm_mix, w_in, g_q_a, g_k_a, g_k_idx, g_cq, w_uq, g_ckv, w_uk, w_uv, g_q_b, g_k_b)
    wuv_b = in_prep[11]
    peer_prep = peer_prepare(w_out, w_peer_q, peer_sub_keys, peer_u, peer_v)

    rot_p = rotary_tables(jnp.arange(s_len, dtype=F32))
    (_qa, ka, va, kiwi, ckv, kpe, ka_b, va_b, ki_b, qi, qb, kb, vb, _) = in_proj(
        x_prompt.reshape(-1, D_MODEL), rot_p, in_prep, tile=IN_TILE)
    seq_p = lambda a: a.reshape(n_p, s_len, a.shape[-1])
    o_p = prompt_attention(seq_p(_qa), seq_p(qi), seq_p(kiwi), seq_p(qb), seq_p(ki_b), seq_p(ka_b), seq_p(va_b),
                           seq_p(kb), seq_p(vb), q_block=Q_BLOCK, n_classes=CAUSAL_CLASSES)
    yp = peer_layer(x_prompt.reshape(-1, D_MODEL), o_p.reshape(-1, D_MIX), g_norm_ffn, peer_prep)
    prompt_rows = (ka.reshape(n_p, s_len, HKV_A, DH_A), va.reshape(n_p, s_len, HKV_A, DH_A),
                   kiwi[:, :D_IDX].reshape(n_p, s_len, D_IDX), ckv.reshape(n_p, s_len, D_C),
                   kpe[:, PE_LANE:PE_LANE + DR_B].reshape(n_p, s_len, DR_B))

    rot_s = rotary_tables(jnp.tile(PAST_LEN + jnp.arange(t_new, dtype=F32), IN_TILE // t_new))
    (_qa, ka, va, kiwi, ckv, kpe, ka_b, va_b, ki_b, qi, qb, kb, vb, qbk) = in_proj(
        x_sample.reshape(-1, D_MODEL), rot_s, in_prep, tile=IN_TILE)
    seq_s = lambda a: a.reshape(n_s, t_new, a.shape[-1])
    o_a = sample_attention_a(page_table, seq_s(_qa), seq_s(qi), seq_s(kiwi), seq_s(ki_b), seq_s(ka_b), seq_s(va_b),
                             cache_a_kidx, cache_a_k, cache_a_v)
    o_b = sample_attention_b(page_table, seq_s(qb), seq_s(qbk), seq_s(kb), seq_s(ckv), w_uk.T.astype(MM_DTYPE), wuv_b,
                             cache_b_ckv, cache_b_kpe)
    o_s = jnp.concatenate([o_a, o_b], axis=-1)
    ys = peer_layer(x_sample.reshape(-1, D_MODEL), o_s.reshape(-1, D_MIX), g_norm_ffn, peer_prep)
    sample_rows = (ka.reshape(n_s, t_new, HKV_A, DH_A), va.reshape(n_s, t_new, HKV_A, DH_A),
                   kiwi[:, :D_IDX].reshape(n_s, t_new, D_IDX), ckv.reshape(n_s, t_new, D_C),
                   kpe[:, PE_LANE:PE_LANE + DR_B].reshape(n_s, t_new, DR_B))
    return (yp.reshape(x_prompt.shape), ys.reshape(x_sample.shape)) + prompt_rows + sample_rows
```
